```python
import math
import jax, jax.numpy as jnp
from jax import lax
import numpy as np

D_MODEL = 2048
BATCH = 16
SEQ = 2048
DEPTH = 2
DEC_BATCH = 8
DEC_SEQ = 32
PAST_LEN = 2048

CHUNK = 64
N_MIXERS = 2
N_ATTN_LAYERS = (DEPTH + 1) // 2
N_POOL_LAYERS = DEPTH // 2
N_HEADS = 8
HEAD_DIM = 128
ROT_DIM = HEAD_DIM // 4
ROPE_THETA = 500000.0
ATTN_WIDTH = N_HEADS * 2 * HEAD_DIM
Q_BLOCK = 128
POOL_WINDOWS = (2, 4, 8, 16)
N_POOL_GROUPS = 4
POOL_GROUP = D_MODEL // N_POOL_GROUPS
POOL_MAX = 16
N_EXPERTS = 64
TOP_K = 8
EXPERT_FF = 512
SHARED_FF = 512
ROUTED_SCALE = 2.5
EXPERT_BLOCK = 128
NORM_EPS = 1e-6
N_ADA = 6

kernel_name = "chunk_streaming_diffattn_pool_moe"


def _rmsnorm(x, g):
    xf = x.astype(jnp.float32)
    y = xf * lax.rsqrt(jnp.mean(xf * xf, axis=-1, keepdims=True) + NORM_EPS)
    return (y * g.astype(jnp.float32)).astype(x.dtype)


def _rope_partial(x, pos):
    half = ROT_DIM // 2
    inv_freq = ROPE_THETA ** (-jnp.arange(half, dtype=jnp.float32) / half)
    ang = pos.astype(jnp.float32)[:, None] * inv_freq[None, :]
    cos = jnp.cos(ang)[None, :, None, :]
    sin = jnp.sin(ang)[None, :, None, :]
    xr = x[..., :ROT_DIM].astype(jnp.float32)
    x1, x2 = xr[..., :half], xr[..., half:]
    rot = jnp.concatenate([x1 * cos - x2 * sin, x2 * cos + x1 * sin], axis=-1)
    return jnp.concatenate([rot.astype(x.dtype), x[..., ROT_DIM:]], axis=-1)


def _diff_attend(q, k, v, lam, mask):
    s = jnp.einsum("bqhmd,bkhmd->bhmqk", q, k).astype(jnp.float32) * (HEAD_DIM ** -0.5)
    s = jnp.where(mask[None, None, None], s, -1e30)
    p = jax.nn.softmax(s, axis=-1)
    a = p[:, :, 0] - lam * p[:, :, 1]
    return jnp.einsum("bhqk,bkhd->bqhd", a.astype(v.dtype), v)


def _diff_attention(h, pos, past_k, past_v, w_qkv, w_o, lq1, lk1, lq2, lk2, subln, lam_init):
    b, s, _ = h.shape
    q, k, v = jnp.split(h @ w_qkv, 3, axis=-1)
    q = _rope_partial(q.reshape(b, s, 2 * N_HEADS, HEAD_DIM), pos)
    k = _rope_partial(k.reshape(b, s, 2 * N_HEADS, HEAD_DIM), pos)
    v = v.reshape(b, s, N_HEADS, 2 * HEAD_DIM)
    f32 = jnp.float32
    lam = (jnp.exp(jnp.sum(lq1.astype(f32) * lk1.astype(f32)))
           - jnp.exp(jnp.sum(lq2.astype(f32) * lk2.astype(f32))) + lam_init)
    q5 = q.reshape(b, s, N_HEADS, 2, HEAD_DIM)
    if past_k is None:
        k5 = k.reshape(b, s, N_HEADS, 2, HEAD_DIM)
        outs = []
        for qb in range(s // Q_BLOCK):
            st, end = qb * Q_BLOCK, (qb + 1) * Q_BLOCK
            qpos = jnp.arange(st, end)
            kpos = jnp.arange(end)
            mask = (kpos[None, :] // CHUNK) <= (qpos[:, None] // CHUNK)
            outs.append(_diff_attend(q5[:, st:end], k5[:, :end], v[:, :end], lam, mask))
        o = jnp.concatenate(outs, axis=1)
    else:
        k_all = jnp.concatenate([past_k.astype(k.dtype), k], axis=1)
        v_all = jnp.concatenate([past_v.astype(v.dtype), v], axis=1)
        sk = k_all.shape[1]
        mask = jnp.ones((s, sk), dtype=bool)
        o = _diff_attend(q5, k_all.reshape(b, sk, N_HEADS, 2, HEAD_DIM), v_all, lam, mask)
    o = _rmsnorm(o, subln) * (1.0 - lam_init)
    y = o.reshape(b, s, ATTN_WIDTH) @ w_o
    return y, k, v


def _pool_mix(h_ext, pos0, w_pool, scale):
    w1 = POOL_MAX - 1
    b, l, d = h_ext.shape
    s = l - w1
    hf = h_ext.astype(jnp.float32)
    csum = jnp.concatenate([jnp.zeros((b, 1, d), jnp.float32), jnp.cumsum(hf, axis=1)], axis=1)
    pos = pos0 + jnp.arange(s, dtype=jnp.int32)
    diffs = []
    for g, w in enumerate(POOL_WINDOWS):
        c0, c1 = g * POOL_GROUP, (g + 1) * POOL_GROUP
        total = csum[:, w1 + 1:w1 + 1 + s, c0:c1] - csum[:, w1 + 1 - w:w1 + 1 - w + s, c0:c1]
        cnt = jnp.minimum(w, pos + 1).astype(jnp.float32)
        diffs.append(total / cnt[None, :, None] - hf[:, w1:, c0:c1])
    dlt = jnp.stack(diffs, axis=2)
    y = jnp.einsum("bsgc,gce->bsge", dlt.astype(h_ext.dtype), w_pool).reshape(b, s, d)
    return y * scale


def _swiglu(x, w_gu, w_down):
    a, u = jnp.split(x @ w_gu, 2, axis=-1)
    return (jax.nn.silu(a) * u) @ w_down


def _moe(h, w_router, router_bias, w_exp_gu, w_exp_down, w_sh_gu, w_sh_down):
    t, d = h.shape
    f32 = jnp.float32
    scores = jax.nn.sigmoid(h.astype(f32) @ w_router.astype(f32))
    _, idx = lax.top_k(scores + router_bias.astype(f32), TOP_K)
    g = jnp.take_along_axis(scores, idx, axis=-1)
    g = g / jnp.sum(g, axis=-1, keepdims=True) * ROUTED_SCALE
    n_rows = t * TOP_K
    flat_e = idx.reshape(-1)
    flat_tok = jnp.arange(n_rows, dtype=jnp.int32) // TOP_K
    flat_g = g.reshape(-1)
    order = jnp.argsort(flat_e)
    sorted_e = flat_e[order]
    counts = jnp.bincount(flat_e, length=N_EXPERTS)
    padded = (counts + EXPERT_BLOCK - 1) // EXPERT_BLOCK * EXPERT_BLOCK
    start = jnp.cumsum(counts) - counts
    pad_end = jnp.cumsum(padded)
    pad_start = pad_end - padded
    dest = pad_start[sorted_e] + jnp.arange(n_rows, dtype=jnp.int32) - start[sorted_e]
    n_blocks = -(-(n_rows + N_EXPERTS * (EXPERT_BLOCK - 1)) // EXPERT_BLOCK)
    n_buf = n_blocks * EXPERT_BLOCK
    tok_buf = jnp.full((n_buf,), t, jnp.int32).at[dest].set(flat_tok[order])
    gate_buf = jnp.zeros((n_buf,), f32).at[dest].set(flat_g[order])
    block_e = jnp.minimum(jnp.searchsorted(pad_end, jnp.arange(n_blocks, dtype=jnp.int32) * EXPERT_BLOCK,
                                           side="right"), N_EXPERTS - 1)
    h_pad = jnp.concatenate([h, jnp.zeros((1, d), h.dtype)], axis=0)

    def expert_block(args):
        tok_b, e_b = args
        return _swiglu(h_pad[tok_b], w_exp_gu[e_b], w_exp_down[e_b])

    ys = lax.map(expert_block, (tok_buf.reshape(n_blocks, EXPERT_BLOCK), block_e))
    ys = ys.reshape(n_buf, d) * gate_buf[:, None].astype(h.dtype)
    routed = jnp.zeros((t + 1, d), h.dtype).at[tok_buf].add(ys)[:t]
    return routed + _swiglu(h, w_sh_gu, w_sh_down)


def setup_inputs(seed: int = 0) -> dict:
    key = jax.random.key(seed)
    ks = jax.random.split(key, 27)
    f32 = jnp.float32

    def nrm(k, shape, scale):
        return jax.random.normal(k, shape, f32) * scale

    return {
        "x_prompt": nrm(ks[0], (BATCH, SEQ, D_MODEL), 1.0),
        "x_sample": nrm(ks[1], (DEC_BATCH, DEC_SEQ, D_MODEL), 1.0),
        "c_prompt": nrm(ks[2], (BATCH, D_MODEL), 1.0),
        "c_sample": nrm(ks[3], (DEC_BATCH, D_MODEL), 1.0),
        "cache_k": nrm(ks[4], (N_ATTN_LAYERS, DEC_BATCH, PAST_LEN, 2 * N_HEADS, HEAD_DIM), 1.0),
        "cache_v": nrm(ks[5], (N_ATTN_LAYERS, DEC_BATCH, PAST_LEN, N_HEADS, 2 * HEAD_DIM), 1.0),
        "state_pool": nrm(ks[6], (N_POOL_LAYERS, DEC_BATCH, POOL_MAX - 1, D_MODEL), 1.0),
        "w_ada": nrm(ks[7], (DEPTH, D_MODEL, N_ADA * D_MODEL), 0.5 * D_MODEL ** -0.5),
        "b_ada": nrm(ks[8], (DEPTH, N_ADA * D_MODEL), 0.02),
        "norm_mix": 1.0 + nrm(ks[9], (DEPTH, D_MODEL), 0.02),
        "norm_ffn": 1.0 + nrm(ks[10], (DEPTH, D_MODEL), 0.02),
        "norm_final": 1.0 + nrm(ks[11], (D_MODEL,), 0.02),
        "w_qkv": nrm(ks[12], (N_ATTN_LAYERS, D_MODEL, 3 * ATTN_WIDTH), D_MODEL ** -0.5),
        "w_o": nrm(ks[13], (N_ATTN_LAYERS, ATTN_WIDTH, D_MODEL), ATTN_WIDTH ** -0.5),
        "lambda_q1": nrm(ks[14], (N_ATTN_LAYERS, HEAD_DIM), 0.1),
        "lambda_k1": nrm(ks[15], (N_ATTN_LAYERS, HEAD_DIM), 0.1),
        "lambda_q2": nrm(ks[16], (N_ATTN_LAYERS, HEAD_DIM), 0.1),
        "lambda_k2": nrm(ks[17], (N_ATTN_LAYERS, HEAD_DIM), 0.1),
        "subln_gain": 1.0 + nrm(ks[18], (N_ATTN_LAYERS, 2 * HEAD_DIM), 0.02),
        "w_pool": nrm(ks[19], (N_POOL_LAYERS, N_POOL_GROUPS, POOL_GROUP, POOL_GROUP), POOL_GROUP ** -0.5),
        "pool_scale": 0.5 + nrm(ks[20], (N_POOL_LAYERS, D_MODEL), 0.05),
        "w_router": nrm(ks[21], (DEPTH, D_MODEL, N_EXPERTS), D_MODEL ** -0.5),
        "router_bias": nrm(ks[22], (DEPTH, N_EXPERTS), 0.01),
        "w_exp_gu": nrm(ks[23], (DEPTH, N_EXPERTS, D_MODEL, 2 * EXPERT_FF), D_MODEL ** -0.5),
        "w_exp_down": nrm(ks[24], (DEPTH, N_EXPERTS, EXPERT_FF, D_MODEL), EXPERT_FF ** -0.5),
        "w_sh_gu": nrm(ks[25], (DEPTH, D_MODEL, 2 * SHARED_FF), D_MODEL ** -0.5),
        "w_sh_down": nrm(ks[26], (DEPTH, SHARED_FF, D_MODEL), SHARED_FF ** -0.5),
    }


def reference(x_prompt, x_sample, c_prompt, c_sample, cache_k, cache_v, state_pool,
              w_ada, b_ada, norm_mix, norm_ffn, norm_final,
              w_qkv, w_o, lambda_q1, lambda_k1, lambda_q2, lambda_k2, subln_gain,
              w_pool, pool_scale,
              w_router, router_bias, w_exp_gu, w_exp_down, w_sh_gu, w_sh_down):

    def trunk(x, c, pos0, past_k, past_v, past_pool):
        b, s, d = x.shape
        pos = pos0 + jnp.arange(s, dtype=jnp.int32)
        c_act = jax.nn.silu(c.astype(jnp.float32))
        new_k, new_v, new_pool = [], [], []
        for i in range(DEPTH):
            mod = (c_act @ w_ada[i].astype(jnp.float32) + b_ada[i].astype(jnp.float32)).astype(x.dtype)[:, None, :]
            sh1, sc1, g1, sh2, sc2, g2 = jnp.split(mod, N_ADA, axis=-1)
            h = _rmsnorm(x, norm_mix[i]) * (1 + sc1) + sh1
            if i % N_MIXERS == 0:
                a = i // N_MIXERS
                lam_init = 0.8 - 0.6 * math.exp(-0.3 * i)
                y, k_rows, v_rows = _diff_attention(
                    h, pos,
                    None if past_k is None else past_k[a],
                    None if past_v is None else past_v[a],
                    w_qkv[a], w_o[a], lambda_q1[a], lambda_k1[a], lambda_q2[a], lambda_k2[a],
                    subln_gain[a], lam_init)
                new_k.append(k_rows)
                new_v.append(v_rows)
            else:
                p = i // N_MIXERS
                if past_pool is None:
                    prev = jnp.zeros((b, POOL_MAX - 1, d), h.dtype)
                else:
                    prev = past_pool[p].astype(h.dtype)
                h_ext = jnp.concatenate([prev, h], axis=1)
                y = _pool_mix(h_ext, pos0, w_pool[p], pool_scale[p])
                new_pool.append(h_ext[:, -(POOL_MAX - 1):])
            x = x + g1 * y
            h = _rmsnorm(x, norm_ffn[i]) * (1 + sc2) + sh2
            x = x + g2 * _moe(h.reshape(b * s, d), w_router[i], router_bias[i], w_exp_gu[i], w_exp_down[i],
                              w_sh_gu[i], w_sh_down[i]).reshape(b, s, d)
        return _rmsnorm(x, norm_final), jnp.stack(new_k), jnp.stack(new_v), jnp.stack(new_pool)

    y_prompt, k_p, v_p, pool_p = trunk(x_prompt, c_prompt, 0, None, None, None)
    y_sample, k_s, v_s, pool_s = trunk(x_sample, c_sample, cache_k.shape[2], cache_k, cache_v, state_pool)
    return (y_prompt, y_sample, k_p, v_p, pool_p, k_s, v_s, pool_s)
```

```python
import functools
import math

import jax
import jax.numpy as jnp
from jax import lax
from jax.experimental import pallas as pl
from jax.experimental.pallas import tpu as pltpu

CHUNK = 64
HEAD_DIM = 128
ROT_DIM = HEAD_DIM // 4
ROPE_THETA = 500000.0
POOL_WINDOWS = (2, 4, 8, 16)
POOL_MAX = 16
TOP_K = 8
ROUTED_SCALE = 2.5
NORM_EPS = 1e-6
N_ADA = 6
N_MIXERS = 2

LANES = 128
MIB = 1024 * 1024

F32 = jnp.float32
BF16 = jnp.bfloat16
NT_DIMS = (((1,), (1,)), ((), ()))


def _cparams(sem, vmem_mib):
    return pltpu.CompilerParams(dimension_semantics=sem, vmem_limit_bytes=vmem_mib * MIB)


def _rms(x, gain):
    return x * lax.rsqrt(jnp.mean(x * x, axis=-1, keepdims=True) + NORM_EPS) * gain


def _norm_mod(x, gain, scale, shift):
    return _rms(x, gain) * (1.0 + scale) + shift


def _swiglu(x_bf, w_gu, w_down, ff):
    au = jnp.dot(x_bf, w_gu, preferred_element_type=F32)
    act = (jax.nn.silu(au[:, :ff]) * au[:, ff:]).astype(BF16)
    return jnp.dot(act, w_down, preferred_element_type=F32)


def _ada_kernel(c_ref, w_ref, b_ref, o_ref):
    c_act = jax.nn.silu(c_ref[...]).astype(BF16)
    o_ref[0] = jnp.dot(c_act, w_ref[0].astype(BF16), preferred_element_type=F32) + b_ref[0]


def _ada(c_all, w_ada, b_ada, tn=1024):
    depth, d, n = w_ada.shape
    r = c_all.shape[0]
    return pl.pallas_call(
        _ada_kernel,
        grid=(depth, n // tn),
        in_specs=[pl.BlockSpec((r, d), lambda i, j: (0, 0)),
                  pl.BlockSpec((1, d, tn), lambda i, j: (i, 0, j)),
                  pl.BlockSpec((1, 1, tn), lambda i, j: (i, 0, j))],
        out_specs=pl.BlockSpec((1, r, tn), lambda i, j: (i, 0, j)),
        out_shape=jax.ShapeDtypeStruct((depth, r, n), F32),
        compiler_params=_cparams(("arbitrary", "arbitrary"), 40),
        name="ada_mod",
    )(c_all, w_ada, b_ada.reshape(depth, 1, n))


def _qkv_kernel(x_ref, mod_ref, g_ref, w_ref, cos_ref, sa_ref, sb_ref, q_ref, k_ref, v_ref, h_scr, *, n_w, tn):
    j = pl.program_id(1)

    @pl.when(j == 0)
    def _():
        h_scr[...] = _norm_mod(x_ref[...], g_ref[...], mod_ref[0, 1:2, :], mod_ref[0, 0:1, :]).astype(BF16)

    acc = jnp.dot(h_scr[...], w_ref[...], preferred_element_type=F32)

    def rope_store(dst):
        cos, sa, sb = cos_ref[...], sa_ref[...], sb_ref[...]
        for c in range(tn // HEAD_DIM):
            blk = acc[:, c * HEAD_DIM:(c + 1) * HEAD_DIM]
            rot = blk * cos + pltpu.roll(blk, HEAD_DIM - ROT_DIM // 2, 1) * sa + pltpu.roll(blk, ROT_DIM // 2, 1) * sb
            dst[:, c * HEAD_DIM:(c + 1) * HEAD_DIM] = rot.astype(dst.dtype)

    @pl.when(j < n_w)
    def _():
        rope_store(q_ref)

    @pl.when((j >= n_w) & (j < 2 * n_w))
    def _():
        rope_store(k_ref)

    @pl.when(j >= 2 * n_w)
    def _():
        v_ref[...] = acc


def _qkv(x2d, mod, gain, w_bf, tabs, seq, tm, tn=512):
    t, d = x2d.shape
    w3 = w_bf.shape[1]
    width = w3 // 3
    n_w = width // tn
    spt = seq // tm
    cos_t, sa_t, sb_t = tabs
    tab_spec = pl.BlockSpec((tm, HEAD_DIM), lambda i, j: (i % spt, 0))
    return pl.pallas_call(
        functools.partial(_qkv_kernel, n_w=n_w, tn=tn),
        grid=(t // tm, w3 // tn),
        in_specs=[pl.BlockSpec((tm, d), lambda i, j: (i, 0)),
                  pl.BlockSpec((1, N_ADA, d), lambda i, j: (i // spt, 0, 0)),
                  pl.BlockSpec((1, d), lambda i, j: (0, 0)),
                  pl.BlockSpec((d, tn), lambda i, j: (0, j)),
                  tab_spec, tab_spec, tab_spec],
        out_specs=[pl.BlockSpec((tm, tn), lambda i, j: (i, jnp.minimum(j, n_w - 1))),
                   pl.BlockSpec((tm, tn), lambda i, j: (i, jnp.clip(j - n_w, 0, n_w - 1))),
                   pl.BlockSpec((tm, tn), lambda i, j: (i, jnp.maximum(j - 2 * n_w, 0)))],
        out_shape=[jax.ShapeDtypeStruct((t, width), BF16),
                   jax.ShapeDtypeStruct((t, width), F32),
                   jax.ShapeDtypeStruct((t, width), F32)],
        scratch_shapes=[pltpu.VMEM((tm, d), BF16)],
        compiler_params=_cparams(("arbitrary", "arbitrary"), 48),
        name="qkv_rope",
    )(x2d, mod, gain.reshape(1, d), w_bf, cos_t, sa_t, sb_t)


def _rope_tables(pos):
    half = ROT_DIM // 2
    inv_freq = ROPE_THETA ** (-jnp.arange(half, dtype=F32) / half)
    ang = pos.astype(F32)[:, None] * inv_freq[None, :]
    cos, sin = jnp.cos(ang), jnp.sin(ang)
    s = pos.shape[0]
    rest = HEAD_DIM - ROT_DIM
    cos_t = jnp.concatenate([cos, cos, jnp.ones((s, rest), F32)], axis=1)
    sa_t = jnp.concatenate([-sin, jnp.zeros((s, half + rest), F32)], axis=1)
    sb_t = jnp.concatenate([jnp.zeros((s, half), F32), sin, jnp.zeros((s, rest), F32)], axis=1)
    return cos_t, sa_t, sb_t


def _lam_value(lam_ref, lam_init):
    l1 = jnp.sum(lam_ref[0:1, :] * lam_ref[1:2, :], axis=1, keepdims=True)
    l2 = jnp.sum(lam_ref[2:3, :] * lam_ref[3:4, :], axis=1, keepdims=True)
    return jnp.exp(l1) - jnp.exp(l2) + lam_init


def _softmax_parts(parts, scale):
    mx = None
    for s in parts:
        m = jnp.max(s, axis=1, keepdims=True)
        mx = m if mx is None else jnp.maximum(mx, m)
    ps = [jnp.exp((s - mx) * scale) for s in parts]
    l = None
    for p in ps:
        r = jnp.sum(p, axis=1, keepdims=True)
        l = r if l is None else l + r
    return ps, l


def _subln(o, sub_ref, lam_init):
    return _rms(o, sub_ref[...]) * (1.0 - lam_init)


def _attn_prompt_kernel(q_ref, k_ref, v_ref, lam_ref, sub_ref, o_ref, kb, vb, *, tq, lam_init):
    seq = q_ref.shape[0]
    hd = HEAD_DIM
    scale = hd ** -0.5
    kb[...] = k_ref[...].astype(BF16)
    vb[...] = v_ref[...].astype(BF16)
    lam = _lam_value(lam_ref, lam_init)
    row = lax.broadcasted_iota(jnp.int32, (tq, tq), 0) // CHUNK
    col = lax.broadcasted_iota(jnp.int32, (tq, tq), 1) // CHUNK
    diag_mask = col <= row
    for qb in range(seq // tq):
        st, end = qb * tq, (qb + 1) * tq
        probs, coef = [], []
        for m in range(2):
            qm = q_ref[st:end, m * hd:(m + 1) * hd]
            km = kb[0:end, m * hd:(m + 1) * hd]
            s = lax.dot_general(qm, km, NT_DIMS, preferred_element_type=F32)
            parts = [s[:, :st]] if st > 0 else []
            parts.append(jnp.where(diag_mask, s[:, st:], -1e30))
            ps, l = _softmax_parts(parts, scale)
            probs.append(ps)
            coef.append((1.0 / l) if m == 0 else (lam / l))
        o = None
        bounds = ([(0, st)] if st > 0 else []) + [(st, end)]
        for idx, (lo, hi) in enumerate(bounds):
            a = (probs[0][idx] * coef[0] - probs[1][idx] * coef[1]).astype(BF16)
            part = jnp.dot(a, vb[lo:hi, :], preferred_element_type=F32)
            o = part if o is None else o + part
        o_ref[st:end, :] = _subln(o, sub_ref, lam_init).astype(o_ref.dtype)


def _attn_prompt(q, k, v, lam_vecs, subln, batch, seq, lam_init, tq=256):
    t, width = q.shape
    hw = 2 * HEAD_DIM
    spec = pl.BlockSpec((seq, hw), lambda b, h: (b, h))
    return pl.pallas_call(
        functools.partial(_attn_prompt_kernel, tq=tq, lam_init=lam_init),
        grid=(batch, width // hw),
        in_specs=[spec, spec, spec,
                  pl.BlockSpec((4, HEAD_DIM), lambda b, h: (0, 0)),
                  pl.BlockSpec((1, hw), lambda b, h: (0, 0))],
        out_specs=spec,
        out_shape=jax.ShapeDtypeStruct((t, width), BF16),
        scratch_shapes=[pltpu.VMEM((seq, hw), BF16), pltpu.VMEM((seq, hw), BF16)],
        compiler_params=_cparams(("arbitrary", "arbitrary"), 56),
        name="attn_prompt",
    )(q, k, v, lam_vecs, subln.reshape(1, hw))


def _attn_sample_kernel(q_ref, k_ref, v_ref, pk_ref, pv_ref, lam_ref, sub_ref, o_ref, *, lam_init):
    hd = HEAD_DIM
    scale = hd ** -0.5
    lam = _lam_value(lam_ref, lam_init)
    pk = pk_ref[...].astype(BF16)
    kn = k_ref[...].astype(BF16)
    probs, coef = [], []
    for m in range(2):
        qm = q_ref[:, m * hd:(m + 1) * hd]
        parts = [lax.dot_general(qm, pk[:, m * hd:(m + 1) * hd], NT_DIMS, preferred_element_type=F32),
                 lax.dot_general(qm, kn[:, m * hd:(m + 1) * hd], NT_DIMS, preferred_element_type=F32)]
        ps, l = _softmax_parts(parts, scale)
        probs.append(ps)
        coef.append((1.0 / l) if m == 0 else (lam / l))
    vals = [pv_ref[...].astype(BF16), v_ref[...].astype(BF16)]
    o = None
    for idx in range(2):
        a = (probs[0][idx] * coef[0] - probs[1][idx] * coef[1]).astype(BF16)
        part = jnp.dot(a, vals[idx], preferred_element_type=F32)
        o = part if o is None else o + part
    o_ref[...] = _subln(o, sub_ref, lam_init).astype(o_ref.dtype)


def _attn_sample(q, k, v, past_k, past_v, lam_vecs, subln, batch, seq, lam_init):
    t, width = q.shape
    hw = 2 * HEAD_DIM
    past = past_k.shape[0] // batch
    spec = pl.BlockSpec((seq, hw), lambda b, h: (b, h))
    pspec = pl.BlockSpec((past, hw), lambda b, h: (b, h))
    return pl.pallas_call(
        functools.partial(_attn_sample_kernel, lam_init=lam_init),
        grid=(batch, width // hw),
        in_specs=[spec, spec, spec, pspec, pspec,
                  pl.BlockSpec((4, HEAD_DIM), lambda b, h: (0, 0)),
                  pl.BlockSpec((1, hw), lambda b, h: (0, 0))],
        out_specs=spec,
        out_shape=jax.ShapeDtypeStruct((t, width), BF16),
        compiler_params=_cparams(("arbitrary", "arbitrary"), 48),
        name="attn_sample",
    )(q, k, v, past_k, past_v, lam_vecs, subln.reshape(1, hw))


def _post_mixer(x, y, mod_ref, gf_ref, wr_ref, x1_ref, h2_ref, lg_ref):
    x1 = x + mod_ref[0, 2:3, :] * y
    h2 = _norm_mod(x1, gf_ref[...], mod_ref[0, 4:5, :], mod_ref[0, 3:4, :])
    x1_ref[...] = x1
    h2_ref[...] = h2
    lg = lax.dot_general(wr_ref[...], h2, NT_DIMS, precision=lax.Precision.HIGHEST, preferred_element_type=F32)
    lg_ref[...] = lg.reshape(lg_ref.shape)


def _logit_out(n_e, t, tm, tile_index):
    if tm % LANES == 0:
        spec = pl.BlockSpec((n_e, tm), lambda *g: (0, tile_index(*g)))
        return jax.ShapeDtypeStruct((n_e, t), F32), spec, lambda a: a
    spec = pl.BlockSpec((1, n_e, tm), lambda *g: (tile_index(*g), 0, 0))
    return (jax.ShapeDtypeStruct((t // tm, n_e, tm), F32), spec,
            lambda a: a.transpose(1, 0, 2).reshape(n_e, t))


def _wo_kernel(o_ref, w_ref, x_ref, mod_ref, gf_ref, wr_ref, x1_ref, h2_ref, lg_ref):
    y = jnp.dot(o_ref[...], w_ref[...], preferred_element_type=F32)
    _post_mixer(x_ref[...], y, mod_ref, gf_ref, wr_ref, x1_ref, h2_ref, lg_ref)


def _wo_post(o, w_bf, x2d, mod, gain_ffn, w_rt, seq, tm):
    t, d = x2d.shape
    width = o.shape[1]
    e = w_rt.shape[0]
    spt = seq // tm
    row = pl.BlockSpec((tm, d), lambda i: (i, 0))
    lg_shape, lg_spec, lg_fix = _logit_out(e, t, tm, lambda i: i)
    x1, h2, lg = pl.pallas_call(
        _wo_kernel,
        grid=(t // tm,),
        in_specs=[pl.BlockSpec((tm, width), lambda i: (i, 0)),
                  pl.BlockSpec((width, d), lambda i: (0, 0)),
                  row,
                  pl.BlockSpec((1, N_ADA, d), lambda i: (i // spt, 0, 0)),
                  pl.BlockSpec((1, d), lambda i: (0, 0)),
                  pl.BlockSpec((e, d), lambda i: (0, 0))],
        out_specs=[row, row, lg_spec],
        out_shape=[jax.ShapeDtypeStruct((t, d), F32), jax.ShapeDtypeStruct((t, d), F32), lg_shape],
        compiler_params=_cparams(("arbitrary",), 48),
        name="wo_post",
    )(o, w_bf, x2d, mod, gain_ffn.reshape(1, d), w_rt)
    return x1, h2, lg_fix(lg)


def _pool_kernel(x_ref, xh_ref, prev_ref, mod_ref, gm_ref, wp_ref, ps_ref, gf_ref, wr_ref,
                 x1_ref, h2_ref, lg_ref, np_ref, hext, ybuf, *, pos0, tm):
    s = pl.program_id(1)
    halo = POOL_MAX
    x = x_ref[...]
    sc1, sh1 = mod_ref[0, 1:2, :], mod_ref[0, 0:1, :]
    h = _norm_mod(x, gm_ref[...], sc1, sh1)
    h_halo = _norm_mod(xh_ref[...], gm_ref[...], sc1, sh1)
    hext[0:halo, :] = jnp.where(s == 0, prev_ref[0], h_halo)
    hext[halo:, :] = h
    pos = pos0 + s * tm + lax.broadcasted_iota(jnp.int32, (tm, 1), 0)
    gc = x.shape[1] // len(POOL_WINDOWS)
    for g, w in enumerate(POOL_WINDOWS):
        c0, c1 = g * gc, (g + 1) * gc
        tot = hext[halo:halo + tm, c0:c1]
        for k in range(1, w):
            tot = tot + hext[halo - k:halo - k + tm, c0:c1]
        cnt = jnp.minimum(w, pos + 1).astype(F32)
        dlt = tot / cnt - hext[halo:halo + tm, c0:c1]
        yg = jnp.dot(dlt.astype(BF16), wp_ref[g], preferred_element_type=F32)
        ybuf[:, c0:c1] = yg * ps_ref[:, c0:c1]
    _post_mixer(x, ybuf[...], mod_ref, gf_ref, wr_ref, x1_ref, h2_ref, lg_ref)

    @pl.when(s == pl.num_programs(1) - 1)
    def _():
        np_ref[0] = hext[tm:tm + halo, :]


def _pool_post(x2d, prev, mod, gain_mix, wp_bf, pool_scale, gain_ffn, w_rt, batch, seq, pos0, tm):
    t, d = x2d.shape
    e = w_rt.shape[0]
    spt = seq // tm
    hpt = tm // POOL_MAX
    ng, gc = wp_bf.shape[0], wp_bf.shape[1]
    row = pl.BlockSpec((tm, d), lambda b, s: (b * spt + s, 0))
    vec = pl.BlockSpec((1, d), lambda b, s: (0, 0))
    lg_shape, lg_spec, lg_fix = _logit_out(e, t, tm, lambda b, s: b * spt + s)
    x1, h2, lg, new_pool = pl.pallas_call(
        functools.partial(_pool_kernel, pos0=pos0, tm=tm),
        grid=(batch, spt),
        in_specs=[row,
                  pl.BlockSpec((POOL_MAX, d), lambda b, s: (jnp.maximum((b * spt + s) * hpt - 1, 0), 0)),
                  pl.BlockSpec((1, POOL_MAX, d), lambda b, s: (b, 0, 0)),
                  pl.BlockSpec((1, N_ADA, d), lambda b, s: (b, 0, 0)),
                  vec,
                  pl.BlockSpec((ng, gc, gc), lambda b, s: (0, 0, 0)),
                  vec, vec,
                  pl.BlockSpec((e, d), lambda b, s: (0, 0))],
        out_specs=[row, row, lg_spec, pl.BlockSpec((1, POOL_MAX, d), lambda b, s: (b, 0, 0))],
        out_shape=[jax.ShapeDtypeStruct((t, d), F32), jax.ShapeDtypeStruct((t, d), F32), lg_shape,
                   jax.ShapeDtypeStruct((batch, POOL_MAX, d), F32)],
        scratch_shapes=[pltpu.VMEM((POOL_MAX + tm, d), F32), pltpu.VMEM((tm, d), F32)],
        compiler_params=_cparams(("arbitrary", "arbitrary"), 48),
        name="pool_post",
    )(x2d, x2d, prev, mod, gain_mix.reshape(1, d), wp_bf, pool_scale.reshape(1, d), gain_ffn.reshape(1, d), w_rt)
    return x1, h2, lg_fix(lg), new_pool


def _route_kernel(lg_ref, bias_ref, idx_ref, gate_ref, rank_ref, cnt_ref, carry):
    i = pl.program_id(0)

    @pl.when(i == 0)
    def _():
        carry[...] = jnp.zeros_like(carry)

    score = jax.nn.sigmoid(lg_ref[...])
    n_e, tm = score.shape
    val = score + bias_ref[...]
    e_iota = lax.broadcasted_iota(jnp.int32, (n_e, tm), 0)
    sel = jnp.zeros((n_e, tm), F32)
    idxs, picked = [], []
    for _ in range(TOP_K):
        best = jnp.max(val, axis=0, keepdims=True)
        ix = jnp.min(jnp.where(val == best, e_iota, n_e), axis=0, keepdims=True)
        hit = e_iota == ix
        picked.append(jnp.sum(jnp.where(hit, score, 0.0), axis=0, keepdims=True))
        idxs.append(ix)
        val = jnp.where(hit, -jnp.inf, val)
        sel = jnp.where(hit, 1.0, sel)
    tri = (lax.broadcasted_iota(jnp.int32, (tm, tm), 0) <= lax.broadcasted_iota(jnp.int32, (tm, tm), 1)).astype(BF16)
    incl = jnp.dot(sel.astype(BF16), tri, preferred_element_type=F32)
    excl = incl - sel + carry[:, 0:1]
    total = picked[0]
    for p in picked[1:]:
        total = total + p
    for r in range(TOP_K):
        hit = e_iota == idxs[r]
        idx_ref[r:r + 1, :] = idxs[r]
        gate_ref[r:r + 1, :] = picked[r] / total * ROUTED_SCALE
        rank_ref[r:r + 1, :] = jnp.sum(jnp.where(hit, excl, 0.0), axis=0, keepdims=True).astype(jnp.int32)
    carry[...] = carry[...] + jnp.sum(sel, axis=1, keepdims=True)
    cnt_ref[...] = carry[...]


def _route(logits_t, bias, tm):
    e, t = logits_t.shape
    out = pl.BlockSpec((TOP_K, tm), lambda i: (0, i))
    return pl.pallas_call(
        _route_kernel,
        grid=(t // tm,),
        in_specs=[pl.BlockSpec((e, tm), lambda i: (0, i)), pl.BlockSpec((e, 1), lambda i: (0, 0))],
        out_specs=[out, out, out, pl.BlockSpec((e, LANES), lambda i: (0, 0))],
        out_shape=[jax.ShapeDtypeStruct((TOP_K, t), jnp.int32), jax.ShapeDtypeStruct((TOP_K, t), F32),
                   jax.ShapeDtypeStruct((TOP_K, t), jnp.int32), jax.ShapeDtypeStruct((e, LANES), F32)],
        scratch_shapes=[pltpu.VMEM((e, LANES), F32)],
        compiler_params=_cparams(("arbitrary",), 32),
        name="route_topk",
    )(logits_t, bias.reshape(e, 1))


DISPATCH_CHUNK = 32


def _row_copy(src, src_row, dst, dst_row, sem):
    return pltpu.make_async_copy(src.at[pl.ds(src_row, 1), :], dst.at[pl.ds(dst_row, 1), :], sem)


def _dispatch_kernel(ps_ref, idx_ref, rank_ref, h_ref, xs_ref, sems, *, td):
    base = pl.program_id(0) * td
    n_chunks = td // DISPATCH_CHUNK

    def start_chunk(c):
        def body(t, carry):
            for r in range(TOP_K):
                dst = ps_ref[idx_ref[r, t]] + rank_ref[r, t]
                _row_copy(h_ref, base + t, xs_ref, dst, sems.at[c % 2]).start()
            return carry
        lax.fori_loop(c * DISPATCH_CHUNK, (c + 1) * DISPATCH_CHUNK, body, 0)

    def wait_chunk(c):
        def body(t, carry):
            for r in range(TOP_K):
                _row_copy(h_ref, 0, xs_ref, 0, sems.at[c % 2]).wait()
            return carry
        lax.fori_loop(0, DISPATCH_CHUNK, body, 0)

    for c in range(n_chunks):
        start_chunk(c)
        if c > 0:
            wait_chunk(c - 1)
    wait_chunk(n_chunks - 1)


def _dispatch(h2, idx_t, rank_t, pad_start, n_buf, td):
    t, d = h2.shape
    smem = pl.BlockSpec((TOP_K, td), lambda i, ps: (0, i), memory_space=pltpu.SMEM)
    return pl.pallas_call(
        functools.partial(_dispatch_kernel, td=td),
        grid_spec=pltpu.PrefetchScalarGridSpec(
            num_scalar_prefetch=1,
            grid=(t // td,),
            in_specs=[smem, smem, pl.BlockSpec(memory_space=pl.ANY)],
            out_specs=pl.BlockSpec(memory_space=pl.ANY),
            scratch_shapes=[pltpu.SemaphoreType.DMA((2,))]),
        out_shape=jax.ShapeDtypeStruct((n_buf, d), h2.dtype),
        compiler_params=_cparams(("arbitrary",), 32),
        name="moe_dispatch",
    )(pad_start, idx_t, rank_t, h2)


def _expert_kernel(be_ref, bi_ref, nv_ref, xs_ref, wgu_ref, wdn_ref, ys_ref, *, ff):
    i = pl.program_id(0)
    nv = nv_ref[i]

    @pl.when(nv > 0)
    def _():
        x = xs_ref[...]
        rows = lax.broadcasted_iota(jnp.int32, (x.shape[0], 1), 0)
        x = jnp.where(rows < nv, x, 0.0).astype(BF16)
        ys_ref[...] = _swiglu(x, wgu_ref[0], wdn_ref[0], ff)


def _experts(xs, wgu_bf, wdn_bf, block_e, block_i, block_nv, bm):
    n_buf, d = xs.shape
    ff = wdn_bf.shape[1]
    row = pl.BlockSpec((bm, d), lambda i, be, bi, nv: (bi[i], 0))
    return pl.pallas_call(
        functools.partial(_expert_kernel, ff=ff),
        grid_spec=pltpu.PrefetchScalarGridSpec(
            num_scalar_prefetch=3,
            grid=(n_buf // bm,),
            in_specs=[row,
                      pl.BlockSpec((1, d, 2 * ff), lambda i, be, bi, nv: (be[i], 0, 0)),
                      pl.BlockSpec((1, ff, d), lambda i, be, bi, nv: (be[i], 0, 0))],
            out_specs=row),
        out_shape=jax.ShapeDtypeStruct((n_buf, d), F32),
        compiler_params=_cparams(("arbitrary",), 48),
        name="moe_experts",
    )(block_e, block_i, block_nv, xs, wgu_bf, wdn_bf)


def _combine_kernel(ps_ref, idx_ref, rank_ref, ys_ref, gate_ref, h2_ref, x1_ref, mod_ref, wsg_ref, wsd_ref, gfin_ref,
                    out_ref, buf, sems, *, tc, ff, final_norm, whole_index):
    tok0 = pl.program_id(0) * tc if whole_index else 0

    def start(t, carry):
        for r in range(TOP_K):
            src = ps_ref[idx_ref[r, tok0 + t]] + rank_ref[r, tok0 + t]
            pltpu.make_async_copy(ys_ref.at[pl.ds(src, 1), :], buf.at[r, pl.ds(t, 1), :], sems.at[r]).start()
        return carry
    lax.fori_loop(0, tc, start, 0)

    acc = _swiglu(h2_ref[...].astype(BF16), wsg_ref[...], wsd_ref[...], ff)

    def wait(t, carry):
        for r in range(TOP_K):
            pltpu.make_async_copy(ys_ref.at[pl.ds(0, 1), :], buf.at[r, pl.ds(0, 1), :], sems.at[r]).wait()
        return carry
    lax.fori_loop(0, tc, wait, 0)

    gate = gate_ref[...]
    routed = gate[:, 0:1] * buf[0]
    for r in range(1, TOP_K):
        routed = routed + gate[:, r:r + 1] * buf[r]
    x2 = x1_ref[...] + mod_ref[0, 5:6, :] * (routed + acc)
    if final_norm:
        x2 = _rms(x2, gfin_ref[...])
    out_ref[...] = x2


def _combine(ys, idx_t, rank_t, gates, pad_start, h2, x1, mod, ws_gu_bf, ws_dn_bf, gain_final, seq, tc, final_norm):
    t, d = x1.shape
    ff = ws_dn_bf.shape[0]
    spt = seq // tc
    whole_index = tc % LANES != 0
    if whole_index:
        smem = pl.BlockSpec((TOP_K, t), lambda i, ps: (0, 0), memory_space=pltpu.SMEM)
    else:
        smem = pl.BlockSpec((TOP_K, tc), lambda i, ps: (0, i), memory_space=pltpu.SMEM)
    row = pl.BlockSpec((tc, d), lambda i, ps: (i, 0))
    return pl.pallas_call(
        functools.partial(_combine_kernel, tc=tc, ff=ff, final_norm=final_norm, whole_index=whole_index),
        grid_spec=pltpu.PrefetchScalarGridSpec(
            num_scalar_prefetch=1,
            grid=(t // tc,),
            in_specs=[smem, smem,
                      pl.BlockSpec(memory_space=pl.ANY),
                      pl.BlockSpec((tc, TOP_K), lambda i, ps: (i, 0)),
                      row, row,
                      pl.BlockSpec((1, N_ADA, d), lambda i, ps: (i // spt, 0, 0)),
                      pl.BlockSpec((d, 2 * ff), lambda i, ps: (0, 0)),
                      pl.BlockSpec((ff, d), lambda i, ps: (0, 0)),
                      pl.BlockSpec((1, d), lambda i, ps: (0, 0))],
            out_specs=row,
            scratch_shapes=[pltpu.VMEM((TOP_K, tc, d), F32), pltpu.SemaphoreType.DMA((TOP_K,))]),
        out_shape=jax.ShapeDtypeStruct((t, d), F32),
        compiler_params=_cparams(("arbitrary",), 48),
        name="moe_combine",
    )(pad_start, idx_t, rank_t, ys, gates, h2, x1, mod, ws_gu_bf, ws_dn_bf, gain_final.reshape(1, d))


def _block_tables(counts, bm, n_blocks):
    n_e = counts.shape[0]
    padded = (counts + bm - 1) // bm * bm
    pad_end = jnp.cumsum(padded)
    pad_start = pad_end - padded
    used = pad_end[-1] // bm
    blk = jnp.arange(n_blocks, dtype=jnp.int32)
    src = jnp.minimum(blk, jnp.maximum(used - 1, 0))
    block_e = jnp.minimum(jnp.searchsorted(pad_end, src * bm, side="right"), n_e - 1).astype(jnp.int32)
    valid = jnp.clip(counts[block_e] - (src * bm - pad_start[block_e]), 0, bm)
    block_nv = jnp.where(blk < used, valid, 0).astype(jnp.int32)
    return pad_start.astype(jnp.int32), block_e, src.astype(jnp.int32), block_nv


def _moe(x1, h2, logits_t, mod, bias, wgu_bf, wdn_bf, ws_gu_bf, ws_dn_bf, gain_final, seq, tiles, final_norm):
    t, d = x1.shape
    n_e = logits_t.shape[0]
    bm = tiles["bm"]
    idx_t, gate_t, rank_t, cnt = _route(logits_t, bias, tiles["route"])
    counts = cnt[:, 0].astype(jnp.int32)
    n_blocks = -(-(t * TOP_K + n_e * (bm - 1)) // bm)
    pad_start, block_e, block_i, block_nv = _block_tables(counts, bm, n_blocks)
    xs = _dispatch(h2, idx_t, rank_t, pad_start, n_blocks * bm, tiles["dispatch"])
    ys = _experts(xs, wgu_bf, wdn_bf, block_e, block_i, block_nv, bm)
    return _combine(ys, idx_t, rank_t, gate_t.T, pad_start, h2, x1, mod, ws_gu_bf, ws_dn_bf, gain_final,
                    seq, tiles["combine"], final_norm)


PROMPT_TILES = dict(qkv=512, wo=256, pool=256, route=512, dispatch=256, bm=256, combine=128)
SAMPLE_TILES = dict(qkv=32, wo=32, pool=32, route=256, dispatch=256, bm=64, combine=32)


def _trunk(x, mods, pos0, past_k, past_v, past_pool, p, tiles):
    b, s, d = x.shape
    depth = len(mods)
    x2d = x.reshape(b * s, d)
    tabs = _rope_tables(pos0 + jnp.arange(s, dtype=jnp.int32))
    new_k, new_v, new_pool = [], [], []
    for i in range(depth):
        mod = mods[i]
        w_rt = p["w_router"][i].T
        if i % N_MIXERS == 0:
            a = i // N_MIXERS
            lam_init = 0.8 - 0.6 * math.exp(-0.3 * i)
            lam_vecs = jnp.stack([p["lambda_q1"][a], p["lambda_k1"][a], p["lambda_q2"][a], p["lambda_k2"][a]])
            q, k, v = _qkv(x2d, mod, p["norm_mix"][i], p["w_qkv_bf"][a], tabs, s, tiles["qkv"])
            if past_k is None:
                o = _attn_prompt(q, k, v, lam_vecs, p["subln_gain"][a], b, s, lam_init)
            else:
                width = k.shape[1]
                o = _attn_sample(q, k, v, past_k[a].reshape(-1, width), past_v[a].reshape(-1, width),
                                 lam_vecs, p["subln_gain"][a], b, s, lam_init)
            new_k.append(k.reshape(b, s, width_heads(k), HEAD_DIM))
            new_v.append(v.reshape(b, s, width_heads(v) // 2, 2 * HEAD_DIM))
            x1, h2, lg = _wo_post(o, p["w_o_bf"][a], x2d, mod, p["norm_ffn"][i], w_rt, s, tiles["wo"])
        else:
            pi = i // N_MIXERS
            if past_pool is None:
                prev = jnp.zeros((b, POOL_MAX, d), F32)
            else:
                prev = jnp.concatenate([jnp.zeros((b, 1, d), F32), past_pool[pi]], axis=1)
            x1, h2, lg, npool = _pool_post(x2d, prev, mod, p["norm_mix"][i], p["w_pool_bf"][pi], p["pool_scale"][pi],
                                           p["norm_ffn"][i], w_rt, b, s, pos0, tiles["pool"])
            new_pool.append(npool[:, 1:, :])
        x2d = _moe(x1, h2, lg, mod, p["router_bias"][i], p["w_exp_gu_bf"][i], p["w_exp_down_bf"][i],
                   p["w_sh_gu_bf"][i], p["w_sh_down_bf"][i], p["norm_final"], s, tiles,
                   final_norm=(i == depth - 1))
    return x2d.reshape(b, s, d), jnp.stack(new_k), jnp.stack(new_v), jnp.stack(new_pool)


def width_heads(a):
    return a.shape[1] // HEAD_DIM


def kernel(x_prompt, x_sample, c_prompt, c_sample, cache_k, cache_v, state_pool, w_ada, b_ada, norm_mix, norm_ffn,
           norm_final, w_qkv, w_o, lambda_q1, lambda_k1, lambda_q2, lambda_k2, subln_gain, w_pool, pool_scale,
           w_router, router_bias, w_exp_gu, w_exp_down, w_sh_gu, w_sh_down):
    depth, d = norm_mix.shape
    nb = c_prompt.shape[0]
    mod_all = _ada(jnp.concatenate([c_prompt, c_sample], axis=0), w_ada, b_ada)
    mod_all = mod_all.reshape(depth, -1, N_ADA, d)
    mods_p = [mod_all[i, :nb] for i in range(depth)]
    mods_s = [mod_all[i, nb:] for i in range(depth)]
    p = dict(norm_mix=norm_mix, norm_ffn=norm_ffn, norm_final=norm_final,
             lambda_q1=lambda_q1, lambda_k1=lambda_k1, lambda_q2=lambda_q2, lambda_k2=lambda_k2,
             subln_gain=subln_gain, pool_scale=pool_scale, w_router=w_router, router_bias=router_bias,
             w_qkv_bf=w_qkv.astype(BF16), w_o_bf=w_o.astype(BF16), w_pool_bf=w_pool.astype(BF16),
             w_exp_gu_bf=w_exp_gu.astype(BF16), w_exp_down_bf=w_exp_down.astype(BF16),
             w_sh_gu_bf=w_sh_gu.astype(BF16), w_sh_down_bf=w_sh_down.astype(BF16))
    y_p, k_p, v_p, pool_p = _trunk(x_prompt, mods_p, 0, None, None, None, p, PROMPT_TILES)
    y_s, k_s, v_s, pool_s = _trunk(x_sample, mods_s, cache_k.shape[2], cache_k, cache_v, state_pool, p, SAMPLE_TILES)
    return (y_p, y_s, k_p, v_p, pool_p, k_s, v_s, pool_s)
```

```python
import functools
import math

import jax
import jax.numpy as jnp
from jax import lax
from jax.experimental import pallas as pl
from jax.experimental.pallas import tpu as pltpu

CHUNK = 64
HEAD_DIM = 128
ROT_DIM = HEAD_DIM // 4
ROPE_THETA = 500000.0
POOL_WINDOWS = (2, 4, 8, 16)
POOL_MAX = 16
TOP_K = 8
ROUTED_SCALE = 2.5
NORM_EPS = 1e-6
N_ADA = 6
N_MIXERS = 2

LANES = 128
MIB = 1024 * 1024

F32 = jnp.float32
BF16 = jnp.bfloat16
NT_DIMS = (((1,), (1,)), ((), ()))


def _cparams(sem, vmem_mib):
    return pltpu.CompilerParams(dimension_semantics=sem, vmem_limit_bytes=vmem_mib * MIB)


def _rms(x, gain):
    return x * lax.rsqrt(jnp.mean(x * x, axis=-1, keepdims=True) + NORM_EPS) * gain


def _norm_mod(x, gain, scale, shift):
    return _rms(x, gain) * (1.0 + scale) + shift


def _swiglu(x_bf, w_gu, w_down, ff):
    au = jnp.dot(x_bf, w_gu, preferred_element_type=F32)
    act = (jax.nn.silu(au[:, :ff]) * au[:, ff:]).astype(BF16)
    return jnp.dot(act, w_down, preferred_element_type=F32)


def _ada_kernel(c_ref, w_ref, b_ref, o_ref):
    c_act = jax.nn.silu(c_ref[...]).astype(BF16)
    o_ref[0] = jnp.dot(c_act, w_ref[0].astype(BF16), preferred_element_type=F32) + b_ref[0]


def _ada(c_all, w_ada, b_ada, tn=1024):
    depth, d, n = w_ada.shape
    r = c_all.shape[0]
    return pl.pallas_call(
        _ada_kernel,
        grid=(depth, n // tn),
        in_specs=[pl.BlockSpec((r, d), lambda i, j: (0, 0)),
                  pl.BlockSpec((1, d, tn), lambda i, j: (i, 0, j)),
                  pl.BlockSpec((1, 1, tn), lambda i, j: (i, 0, j))],
        out_specs=pl.BlockSpec((1, r, tn), lambda i, j: (i, 0, j)),
        out_shape=jax.ShapeDtypeStruct((depth, r, n), F32),
        compiler_params=_cparams(("arbitrary", "arbitrary"), 40),
        name="ada_mod",
    )(c_all, w_ada, b_ada.reshape(depth, 1, n))


def _qkv_kernel(x_ref, mod_ref, g_ref, w_ref, cos_ref, sa_ref, sb_ref, q_ref, k_ref, v_ref, h_scr, *, n_w, tn):
    j = pl.program_id(1)

    @pl.when(j == 0)
    def _():
        h_scr[...] = _norm_mod(x_ref[...], g_ref[...], mod_ref[0, 1:2, :], mod_ref[0, 0:1, :]).astype(BF16)

    acc = jnp.dot(h_scr[...], w_ref[...], preferred_element_type=F32)

    def rope_store(dst):
        cos, sa, sb = cos_ref[...], sa_ref[...], sb_ref[...]
        for c in range(tn // HEAD_DIM):
            blk = acc[:, c * HEAD_DIM:(c + 1) * HEAD_DIM]
            rot = blk * cos + pltpu.roll(blk, HEAD_DIM - ROT_DIM // 2, 1) * sa + pltpu.roll(blk, ROT_DIM // 2, 1) * sb
            dst[:, c * HEAD_DIM:(c + 1) * HEAD_DIM] = rot.astype(dst.dtype)

    @pl.when(j < n_w)
    def _():
        rope_store(q_ref)

    @pl.when((j >= n_w) & (j < 2 * n_w))
    def _():
        rope_store(k_ref)

    @pl.when(j >= 2 * n_w)
    def _():
        v_ref[...] = acc


def _qkv(x2d, mod, gain, w_bf, tabs, seq, tm, tn=512):
    t, d = x2d.shape
    w3 = w_bf.shape[1]
    width = w3 // 3
    n_w = width // tn
    spt = seq // tm
    cos_t, sa_t, sb_t = tabs
    tab_spec = pl.BlockSpec((tm, HEAD_DIM), lambda i, j: (i % spt, 0))
    return pl.pallas_call(
        functools.partial(_qkv_kernel, n_w=n_w, tn=tn),
        grid=(t // tm, w3 // tn),
        in_specs=[pl.BlockSpec((tm, d), lambda i, j: (i, 0)),
                  pl.BlockSpec((1, N_ADA, d), lambda i, j: (i // spt, 0, 0)),
                  pl.BlockSpec((1, d), lambda i, j: (0, 0)),
                  pl.BlockSpec((d, tn), lambda i, j: (0, j)),
                  tab_spec, tab_spec, tab_spec],
        out_specs=[pl.BlockSpec((tm, tn), lambda i, j: (i, jnp.minimum(j, n_w - 1))),
                   pl.BlockSpec((tm, tn), lambda i, j: (i, jnp.clip(j - n_w, 0, n_w - 1))),
                   pl.BlockSpec((tm, tn), lambda i, j: (i, jnp.maximum(j - 2 * n_w, 0)))],
        out_shape=[jax.ShapeDtypeStruct((t, width), BF16),
                   jax.ShapeDtypeStruct((t, width), F32),
                   jax.ShapeDtypeStruct((t, width), F32)],
        scratch_shapes=[pltpu.VMEM((tm, d), BF16)],
        compiler_params=_cparams(("arbitrary", "arbitrary"), 48),
        name="qkv_rope",
    )(x2d, mod, gain.reshape(1, d), w_bf, cos_t, sa_t, sb_t)


def _rope_tables(pos):
    half = ROT_DIM // 2
    inv_freq = ROPE_THETA ** (-jnp.arange(half, dtype=F32) / half)
    ang = pos.astype(F32)[:, None] * inv_freq[None, :]
    cos, sin = jnp.cos(ang), jnp.sin(ang)
    s = pos.shape[0]
    rest = HEAD_DIM - ROT_DIM
    cos_t = jnp.concatenate([cos, cos, jnp.ones((s, rest), F32)], axis=1)
    sa_t = jnp.concatenate([-sin, jnp.zeros((s, half + rest), F32)], axis=1)
    sb_t = jnp.concatenate([jnp.zeros((s, half), F32), sin, jnp.zeros((s, rest), F32)], axis=1)
    return cos_t, sa_t, sb_t


def _lam_value(lam_ref, lam_init):
    l1 = jnp.sum(lam_ref[0:1, :] * lam_ref[1:2, :], axis=1, keepdims=True)
    l2 = jnp.sum(lam_ref[2:3, :] * lam_ref[3:4, :], axis=1, keepdims=True)
    return jnp.exp(l1) - jnp.exp(l2) + lam_init


def _softmax_parts(parts, scale):
    mx = None
    for s in parts:
        m = jnp.max(s, axis=1, keepdims=True)
        mx = m if mx is None else jnp.maximum(mx, m)
    ps = [jnp.exp((s - mx) * scale) for s in parts]
    l = None
    for p in ps:
        r = jnp.sum(p, axis=1, keepdims=True)
        l = r if l is None else l + r
    return ps, l


def _subln(o, sub_ref, lam_init):
    return _rms(o, sub_ref[...]) * (1.0 - lam_init)


def _attn_prompt_kernel(q_ref, k_ref, v_ref, lam_ref, sub_ref, o_ref, kb, vb, *, tq, lam_init):
    seq = q_ref.shape[0]
    hd = HEAD_DIM
    scale = hd ** -0.5
    kb[...] = k_ref[...].astype(BF16)
    vb[...] = v_ref[...].astype(BF16)
    lam = _lam_value(lam_ref, lam_init)
    row = lax.broadcasted_iota(jnp.int32, (tq, tq), 0) // CHUNK
    col = lax.broadcasted_iota(jnp.int32, (tq, tq), 1) // CHUNK
    diag_mask = col <= row
    for qb in range(seq // tq):
        st, end = qb * tq, (qb + 1) * tq
        probs, coef = [], []
        for m in range(2):
            qm = q_ref[st:end, m * hd:(m + 1) * hd]
            km = kb[0:end, m * hd:(m + 1) * hd]
            s = lax.dot_general(qm, km, NT_DIMS, preferred_element_type=F32)
            parts = [s[:, :st]] if st > 0 else []
            parts.append(jnp.where(diag_mask, s[:, st:], -1e30))
            ps, l = _softmax_parts(parts, scale)
            probs.append(ps)
            coef.append((1.0 / l) if m == 0 else (lam / l))
        o = None
        bounds = ([(0, st)] if st > 0 else []) + [(st, end)]
        for idx, (lo, hi) in enumerate(bounds):
            a = (probs[0][idx] * coef[0] - probs[1][idx] * coef[1]).astype(BF16)
            part = jnp.dot(a, vb[lo:hi, :], preferred_element_type=F32)
            o = part if o is None else o + part
        o_ref[st:end, :] = _subln(o, sub_ref, lam_init).astype(o_ref.dtype)


def _attn_prompt(q, k, v, lam_vecs, subln, batch, seq, lam_init, tq=256):
    t, width = q.shape
    hw = 2 * HEAD_DIM
    spec = pl.BlockSpec((seq, hw), lambda b, h: (b, h))
    return pl.pallas_call(
        functools.partial(_attn_prompt_kernel, tq=tq, lam_init=lam_init),
        grid=(batch, width // hw),
        in_specs=[spec, spec, spec,
                  pl.BlockSpec((4, HEAD_DIM), lambda b, h: (0, 0)),
                  pl.BlockSpec((1, hw), lambda b, h: (0, 0))],
        out_specs=spec,
        out_shape=jax.ShapeDtypeStruct((t, width), BF16),
        scratch_shapes=[pltpu.VMEM((seq, hw), BF16), pltpu.VMEM((seq, hw), BF16)],
        compiler_params=_cparams(("arbitrary", "arbitrary"), 56),
        name="attn_prompt",
    )(q, k, v, lam_vecs, subln.reshape(1, hw))


def _attn_sample_kernel(q_ref, k_ref, v_ref, pk_ref, pv_ref, lam_ref, sub_ref, o_ref, *, lam_init):
    hd = HEAD_DIM
    scale = hd ** -0.5
    lam = _lam_value(lam_ref, lam_init)
    pk = pk_ref[...].astype(BF16)
    kn = k_ref[...].astype(BF16)
    probs, coef = [], []
    for m in range(2):
        qm = q_ref[:, m * hd:(m + 1) * hd]
        parts = [lax.dot_general(qm, pk[:, m * hd:(m + 1) * hd], NT_DIMS, preferred_element_type=F32),
                 lax.dot_general(qm, kn[:, m * hd:(m + 1) * hd], NT_DIMS, preferred_element_type=F32)]
        ps, l = _softmax_parts(parts, scale)
        probs.append(ps)
        coef.append((1.0 / l) if m == 0 else (lam / l))
    vals = [pv_ref[...].astype(BF16), v_ref[...].astype(BF16)]
    o = None
    for idx in range(2):
        a = (probs[0][idx] * coef[0] - probs[1][idx] * coef[1]).astype(BF16)
        part = jnp.dot(a, vals[idx], preferred_element_type=F32)
        o = part if o is None else o + part
    o_ref[...] = _subln(o, sub_ref, lam_init).astype(o_ref.dtype)


def _attn_sample(q, k, v, past_k, past_v, lam_vecs, subln, batch, seq, lam_init):
    t, width = q.shape
    hw = 2 * HEAD_DIM
    past = past_k.shape[0] // batch
    spec = pl.BlockSpec((seq, hw), lambda b, h: (b, h))
    pspec = pl.BlockSpec((past, hw), lambda b, h: (b, h))
    return pl.pallas_call(
        functools.partial(_attn_sample_kernel, lam_init=lam_init),
        grid=(batch, width // hw),
        in_specs=[spec, spec, spec, pspec, pspec,
                  pl.BlockSpec((4, HEAD_DIM), lambda b, h: (0, 0)),
                  pl.BlockSpec((1, hw), lambda b, h: (0, 0))],
        out_specs=spec,
        out_shape=jax.ShapeDtypeStruct((t, width), BF16),
        compiler_params=_cparams(("arbitrary", "arbitrary"), 48),
        name="attn_sample",
    )(q, k, v, past_k, past_v, lam_vecs, subln.reshape(1, hw))


def _post_mixer(x, y, mod_ref, gf_ref, wr_ref, x1_ref, h2_ref, lg_ref):
    x1 = x + mod_ref[0, 2:3, :] * y
    h2 = _norm_mod(x1, gf_ref[...], mod_ref[0, 4:5, :], mod_ref[0, 3:4, :])
    x1_ref[...] = x1
    h2_ref[...] = h2
    h_hi = h2.astype(BF16)
    h_lo = (h2 - h_hi.astype(F32)).astype(BF16)
    w = wr_ref[...]
    w_hi = w.astype(BF16)
    w_lo = (w - w_hi.astype(F32)).astype(BF16)
    lg_ref[...] = (jnp.dot(h_hi, w_hi, preferred_element_type=F32)
                   + (jnp.dot(h_hi, w_lo, preferred_element_type=F32)
                      + jnp.dot(h_lo, w_hi, preferred_element_type=F32)))


def _wo_kernel(o_ref, w_ref, x_ref, mod_ref, gf_ref, wr_ref, x1_ref, h2_ref, lg_ref):
    y = jnp.dot(o_ref[...], w_ref[...], preferred_element_type=F32)
    _post_mixer(x_ref[...], y, mod_ref, gf_ref, wr_ref, x1_ref, h2_ref, lg_ref)


def _wo_post(o, w_bf, x2d, mod, gain_ffn, w_router, seq, tm):
    t, d = x2d.shape
    width = o.shape[1]
    e = w_router.shape[1]
    spt = seq // tm
    row = pl.BlockSpec((tm, d), lambda i: (i, 0))
    x1, h2, lg = pl.pallas_call(
        _wo_kernel,
        grid=(t // tm,),
        in_specs=[pl.BlockSpec((tm, width), lambda i: (i, 0)),
                  pl.BlockSpec((width, d), lambda i: (0, 0)),
                  row,
                  pl.BlockSpec((1, N_ADA, d), lambda i: (i // spt, 0, 0)),
                  pl.BlockSpec((1, d), lambda i: (0, 0)),
                  pl.BlockSpec((d, e), lambda i: (0, 0))],
        out_specs=[row, row, pl.BlockSpec((tm, e), lambda i: (i, 0))],
        out_shape=[jax.ShapeDtypeStruct((t, d), F32), jax.ShapeDtypeStruct((t, d), F32),
                   jax.ShapeDtypeStruct((t, e), F32)],
        compiler_params=_cparams(("arbitrary",), 48),
        name="wo_post",
    )(o, w_bf, x2d, mod, gain_ffn.reshape(1, d), w_router)
    return x1, h2, lg.T


def _pool_kernel(x_ref, xh_ref, prev_ref, mod_ref, gm_ref, wp_ref, ps_ref, gf_ref, wr_ref,
                 x1_ref, h2_ref, lg_ref, np_ref, hext, ybuf, *, pos0, tm):
    s = pl.program_id(1)
    halo = POOL_MAX
    x = x_ref[...]
    sc1, sh1 = mod_ref[0, 1:2, :], mod_ref[0, 0:1, :]
    h = _norm_mod(x, gm_ref[...], sc1, sh1)
    h_halo = _norm_mod(xh_ref[...], gm_ref[...], sc1, sh1)
    hext[0:halo, :] = jnp.where(s == 0, prev_ref[0], h_halo)
    hext[halo:, :] = h
    pos = pos0 + s * tm + lax.broadcasted_iota(jnp.int32, (tm, 1), 0)
    gc = x.shape[1] // len(POOL_WINDOWS)
    for g, w in enumerate(POOL_WINDOWS):
        c0, c1 = g * gc, (g + 1) * gc
        tot = hext[halo:halo + tm, c0:c1]
        for k in range(1, w):
            tot = tot + hext[halo - k:halo - k + tm, c0:c1]
        cnt = jnp.minimum(w, pos + 1).astype(F32)
        dlt = tot / cnt - hext[halo:halo + tm, c0:c1]
        yg = jnp.dot(dlt.astype(BF16), wp_ref[g], preferred_element_type=F32)
        ybuf[:, c0:c1] = yg * ps_ref[:, c0:c1]
    _post_mixer(x, ybuf[...], mod_ref, gf_ref, wr_ref, x1_ref, h2_ref, lg_ref)

    @pl.when(s == pl.num_programs(1) - 1)
    def _():
        np_ref[0] = hext[tm:tm + halo, :]


def _pool_post(x2d, prev, mod, gain_mix, wp_bf, pool_scale, gain_ffn, w_router, batch, seq, pos0, tm):
    t, d = x2d.shape
    e = w_router.shape[1]
    spt = seq // tm
    hpt = tm // POOL_MAX
    ng, gc = wp_bf.shape[0], wp_bf.shape[1]
    row = pl.BlockSpec((tm, d), lambda b, s: (b * spt + s, 0))
    vec = pl.BlockSpec((1, d), lambda b, s: (0, 0))
    x1, h2, lg, new_pool = pl.pallas_call(
        functools.partial(_pool_kernel, pos0=pos0, tm=tm),
        grid=(batch, spt),
        in_specs=[row,
                  pl.BlockSpec((POOL_MAX, d), lambda b, s: (jnp.maximum((b * spt + s) * hpt - 1, 0), 0)),
                  pl.BlockSpec((1, POOL_MAX, d), lambda b, s: (b, 0, 0)),
                  pl.BlockSpec((1, N_ADA, d), lambda b, s: (b, 0, 0)),
                  vec,
                  pl.BlockSpec((ng, gc, gc), lambda b, s: (0, 0, 0)),
                  vec, vec,
                  pl.BlockSpec((d, e), lambda b, s: (0, 0))],
        out_specs=[row, row, pl.BlockSpec((tm, e), lambda b, s: (b * spt + s, 0)),
                   pl.BlockSpec((1, POOL_MAX, d), lambda b, s: (b, 0, 0))],
        out_shape=[jax.ShapeDtypeStruct((t, d), F32), jax.ShapeDtypeStruct((t, d), F32),
                   jax.ShapeDtypeStruct((t, e), F32), jax.ShapeDtypeStruct((batch, POOL_MAX, d), F32)],
        scratch_shapes=[pltpu.VMEM((POOL_MAX + tm, d), F32), pltpu.VMEM((tm, d), F32)],
        compiler_params=_cparams(("arbitrary", "arbitrary"), 48),
        name="pool_post",
    )(x2d, x2d, prev, mod, gain_mix.reshape(1, d), wp_bf, pool_scale.reshape(1, d), gain_ffn.reshape(1, d), w_router)
    return x1, h2, lg.T, new_pool


def _route_kernel(lg_ref, bias_ref, idx_ref, gate_ref, rank_ref, cnt_ref, carry):
    i = pl.program_id(0)

    @pl.when(i == 0)
    def _():
        carry[...] = jnp.zeros_like(carry)

    score = jax.nn.sigmoid(lg_ref[...])
    n_e, tm = score.shape
    val = score + bias_ref[...]
    e_iota = lax.broadcasted_iota(jnp.int32, (n_e, tm), 0)
    sel = jnp.zeros((n_e, tm), F32)
    idxs, picked = [], []
    for _ in range(TOP_K):
        best = jnp.max(val, axis=0, keepdims=True)
        ix = jnp.min(jnp.where(val == best, e_iota, n_e), axis=0, keepdims=True)
        hit = e_iota == ix
        picked.append(jnp.sum(jnp.where(hit, score, 0.0), axis=0, keepdims=True))
        idxs.append(ix)
        val = jnp.where(hit, -jnp.inf, val)
        sel = jnp.where(hit, 1.0, sel)
    tri = (lax.broadcasted_iota(jnp.int32, (tm, tm), 0) <= lax.broadcasted_iota(jnp.int32, (tm, tm), 1)).astype(BF16)
    incl = jnp.dot(sel.astype(BF16), tri, preferred_element_type=F32)
    excl = incl - sel + carry[:, 0:1]
    total = picked[0]
    for p in picked[1:]:
        total = total + p
    for r in range(TOP_K):
        hit = e_iota == idxs[r]
        idx_ref[r:r + 1, :] = idxs[r]
        gate_ref[r:r + 1, :] = picked[r] / total * ROUTED_SCALE
        rank_ref[r:r + 1, :] = jnp.sum(jnp.where(hit, excl, 0.0), axis=0, keepdims=True).astype(jnp.int32)
    carry[...] = carry[...] + jnp.sum(sel, axis=1, keepdims=True)
    cnt_ref[...] = carry[...]


def _route(logits_t, bias, tm):
    e, t = logits_t.shape
    out = pl.BlockSpec((TOP_K, tm), lambda i: (0, i))
    return pl.pallas_call(
        _route_kernel,
        grid=(t // tm,),
        in_specs=[pl.BlockSpec((e, tm), lambda i: (0, i)), pl.BlockSpec((e, 1), lambda i: (0, 0))],
        out_specs=[out, out, out, pl.BlockSpec((e, LANES), lambda i: (0, 0))],
        out_shape=[jax.ShapeDtypeStruct((TOP_K, t), jnp.int32), jax.ShapeDtypeStruct((TOP_K, t), F32),
                   jax.ShapeDtypeStruct((TOP_K, t), jnp.int32), jax.ShapeDtypeStruct((e, LANES), F32)],
        scratch_shapes=[pltpu.VMEM((e, LANES), F32)],
        compiler_params=_cparams(("arbitrary",), 32),
        name="route_topk",
    )(logits_t, bias.reshape(e, 1))


DISPATCH_CHUNK = 32


def _row_copy(src, src_row, dst, dst_row, sem):
    return pltpu.make_async_copy(src.at[pl.ds(src_row, 1), :], dst.at[pl.ds(dst_row, 1), :], sem)


def _dest_kernel(ps_ref, idx_ref, rank_ref, dest_ref):
    idx = idx_ref[...]
    dest = rank_ref[...]
    for e in range(ps_ref.shape[0]):
        dest = dest + jnp.where(idx == e, ps_ref[e], 0)
    dest_ref[...] = dest


def _dest_rows(idx_t, rank_t, pad_start, tm):
    k, t = idx_t.shape
    blk = pl.BlockSpec((k, tm), lambda i, ps: (0, i))
    return pl.pallas_call(
        _dest_kernel,
        grid_spec=pltpu.PrefetchScalarGridSpec(num_scalar_prefetch=1, grid=(t // tm,), in_specs=[blk, blk],
                                               out_specs=blk),
        out_shape=jax.ShapeDtypeStruct((k, t), jnp.int32),
        compiler_params=_cparams(("arbitrary",), 32),
        name="moe_dest",
    )(pad_start, idx_t, rank_t)


def _pack_bf16_pairs(x):
    half = x.shape[1] // 2
    lo = lax.bitcast_convert_type(x[:, :half].astype(BF16).astype(F32), jnp.uint32)
    hi = lax.bitcast_convert_type(x[:, half:].astype(BF16).astype(F32), jnp.uint32)
    return (lo >> 16) | (hi & jnp.uint32(0xFFFF0000))


def _unpack_bf16_pairs(w):
    lo = lax.bitcast_convert_type(w << 16, F32).astype(BF16)
    hi = lax.bitcast_convert_type(w & jnp.uint32(0xFFFF0000), F32).astype(BF16)
    return lo, hi


def _dispatch_kernel(dest_ref, h_ref, xs_ref, packed, sems, *, td):
    n_chunks = td // DISPATCH_CHUNK
    packed[...] = _pack_bf16_pairs(h_ref[...])

    def start_chunk(c):
        for t in range(c * DISPATCH_CHUNK, (c + 1) * DISPATCH_CHUNK):
            for r in range(TOP_K):
                _row_copy(packed, t, xs_ref, dest_ref[r, t], sems.at[c % 2]).start()

    def wait_chunk(c):
        for _ in range(DISPATCH_CHUNK * TOP_K):
            _row_copy(packed, 0, xs_ref, 0, sems.at[c % 2]).wait()

    for c in range(n_chunks):
        start_chunk(c)
        if c > 0:
            wait_chunk(c - 1)
    wait_chunk(n_chunks - 1)


def _dispatch(h2, dest_t, n_buf, td):
    t, d = h2.shape
    return pl.pallas_call(
        functools.partial(_dispatch_kernel, td=td),
        grid=(t // td,),
        in_specs=[pl.BlockSpec((TOP_K, td), lambda i: (0, i), memory_space=pltpu.SMEM),
                  pl.BlockSpec((td, d), lambda i: (i, 0))],
        out_specs=pl.BlockSpec(memory_space=pl.ANY),
        out_shape=jax.ShapeDtypeStruct((n_buf, d // 2), jnp.uint32),
        scratch_shapes=[pltpu.VMEM((td, d // 2), jnp.uint32), pltpu.SemaphoreType.DMA((2,))],
        compiler_params=_cparams(("arbitrary",), 32),
        name="moe_dispatch",
    )(dest_t, h2)


def _expert_kernel(be_ref, bi_ref, nv_ref, xs_ref, wgu_ref, wdn_ref, ys_ref, wgu_bf, wdn_bf, *, ff):
    i = pl.program_id(0)
    nv = nv_ref[i]

    @pl.when((i == 0) | (be_ref[i] != be_ref[jnp.maximum(i - 1, 0)]))
    def _():
        wgu_bf[...] = wgu_ref[0, 0].astype(BF16)
        wdn_bf[...] = wdn_ref[0, 0].astype(BF16)

    @pl.when(nv > 0)
    def _():
        w = xs_ref[...]
        rows = lax.broadcasted_iota(jnp.int32, (w.shape[0], 1), 0)
        w = jnp.where(rows < nv, w, jnp.uint32(0))
        x_lo, x_hi = _unpack_bf16_pairs(w)
        half = w.shape[1]
        au = (jnp.dot(x_lo, wgu_bf[0:half, :], preferred_element_type=F32)
              + jnp.dot(x_hi, wgu_bf[half:, :], preferred_element_type=F32))
        act = (jax.nn.silu(au[:, :ff]) * au[:, ff:]).astype(BF16)
        ys_ref[...] = jnp.dot(act, wdn_bf[...], preferred_element_type=F32)


def _experts(xs, w_gu, w_down, layer, block_e, block_i, block_nv, bm):
    n_buf = xs.shape[0]
    d, ff = w_gu.shape[2], w_down.shape[2]
    row_in = pl.BlockSpec((bm, d // 2), lambda i, be, bi, nv: (bi[i], 0))
    row = pl.BlockSpec((bm, d), lambda i, be, bi, nv: (bi[i], 0))
    return pl.pallas_call(
        functools.partial(_expert_kernel, ff=ff),
        grid_spec=pltpu.PrefetchScalarGridSpec(
            num_scalar_prefetch=3,
            grid=(n_buf // bm,),
            in_specs=[row_in,
                      pl.BlockSpec((1, 1, d, 2 * ff), lambda i, be, bi, nv: (layer, be[i], 0, 0)),
                      pl.BlockSpec((1, 1, ff, d), lambda i, be, bi, nv: (layer, be[i], 0, 0))],
            out_specs=row,
            scratch_shapes=[pltpu.VMEM((d, 2 * ff), BF16), pltpu.VMEM((ff, d), BF16)]),
        out_shape=jax.ShapeDtypeStruct((n_buf, d), F32),
        compiler_params=_cparams(("arbitrary",), 56),
        name="moe_experts",
    )(block_e, block_i, block_nv, xs, w_gu, w_down)


def _combine_kernel(dest_ref, ys_ref, gate_ref, h2_ref, x1_ref, mod_ref, wsg_ref, wsd_ref, gfin_ref,
                    out_ref, buf, sems, *, tc, ff, final_norm, whole_index):
    tok0 = pl.program_id(0) * tc if whole_index else 0

    for t in range(tc):
        for r in range(TOP_K):
            pltpu.make_async_copy(ys_ref.at[pl.ds(dest_ref[r, tok0 + t], 1), :], buf.at[r, pl.ds(t, 1), :],
                                  sems.at[r]).start()

    acc = _swiglu(h2_ref[...].astype(BF16), wsg_ref[...], wsd_ref[...], ff)

    for r in range(TOP_K):
        for _ in range(tc):
            pltpu.make_async_copy(ys_ref.at[pl.ds(0, 1), :], buf.at[r, pl.ds(0, 1), :], sems.at[r]).wait()

    gate = gate_ref[...]
    routed = gate[:, 0:1] * buf[0]
    for r in range(1, TOP_K):
        routed = routed + gate[:, r:r + 1] * buf[r]
    x2 = x1_ref[...] + mod_ref[0, 5:6, :] * (routed + acc)
    if final_norm:
        x2 = _rms(x2, gfin_ref[...])
    out_ref[...] = x2


def _combine(ys, dest_t, gates, h2, x1, mod, ws_gu_bf, ws_dn_bf, gain_final, seq, tc, final_norm):
    t, d = x1.shape
    ff = ws_dn_bf.shape[0]
    spt = seq // tc
    whole_index = tc % LANES != 0
    if whole_index:
        smem = pl.BlockSpec((TOP_K, t), lambda i: (0, 0), memory_space=pltpu.SMEM)
    else:
        smem = pl.BlockSpec((TOP_K, tc), lambda i: (0, i), memory_space=pltpu.SMEM)
    row = pl.BlockSpec((tc, d), lambda i: (i, 0))
    return pl.pallas_call(
        functools.partial(_combine_kernel, tc=tc, ff=ff, final_norm=final_norm, whole_index=whole_index),
        grid=(t // tc,),
        in_specs=[smem,
                  pl.BlockSpec(memory_space=pl.ANY),
                  pl.BlockSpec((tc, TOP_K), lambda i: (i, 0)),
                  row, row,
                  pl.BlockSpec((1, N_ADA, d), lambda i: (i // spt, 0, 0)),
                  pl.BlockSpec((d, 2 * ff), lambda i: (0, 0)),
                  pl.BlockSpec((ff, d), lambda i: (0, 0)),
                  pl.BlockSpec((1, d), lambda i: (0, 0))],
        out_specs=row,
        out_shape=jax.ShapeDtypeStruct((t, d), F32),
        scratch_shapes=[pltpu.VMEM((TOP_K, tc, d), F32), pltpu.SemaphoreType.DMA((TOP_K,))],
        compiler_params=_cparams(("arbitrary",), 48),
        name="moe_combine",
    )(dest_t, ys, gates, h2, x1, mod, ws_gu_bf, ws_dn_bf, gain_final.reshape(1, d))


def _block_tables(counts, bm, n_blocks):
    n_e = counts.shape[0]
    padded = (counts + bm - 1) // bm * bm
    pad_end = jnp.cumsum(padded)
    pad_start = pad_end - padded
    used = pad_end[-1] // bm
    blk = jnp.arange(n_blocks, dtype=jnp.int32)
    src = jnp.minimum(blk, jnp.maximum(used - 1, 0))
    owner = jnp.sum((pad_end[None, :] <= (src * bm)[:, None]).astype(jnp.int32), axis=1)
    block_e = jnp.minimum(owner, n_e - 1)
    valid = jnp.clip(counts[block_e] - (src * bm - pad_start[block_e]), 0, bm)
    block_nv = jnp.where(blk < used, valid, 0).astype(jnp.int32)
    return pad_start.astype(jnp.int32), block_e.astype(jnp.int32), src.astype(jnp.int32), block_nv


def _moe(x1, h2, logits_t, mod, bias, w_exp_gu, w_exp_down, layer, ws_gu_bf, ws_dn_bf, gain_final, seq, tiles,
         final_norm):
    t, d = x1.shape
    n_e = logits_t.shape[0]
    bm = tiles["bm"]
    idx_t, gate_t, rank_t, cnt = _route(logits_t, bias, tiles["route"])
    counts = cnt[:, 0].astype(jnp.int32)
    n_blocks = -(-(t * TOP_K + n_e * (bm - 1)) // bm)
    pad_start, block_e, block_i, block_nv = _block_tables(counts, bm, n_blocks)
    dest_t = _dest_rows(idx_t, rank_t, pad_start, tiles["route"])
    xs = _dispatch(h2, dest_t, n_blocks * bm, tiles["dispatch"])
    ys = _experts(xs, w_exp_gu, w_exp_down, layer, block_e, block_i, block_nv, bm)
    return _combine(ys, dest_t, gate_t.T, h2, x1, mod, ws_gu_bf, ws_dn_bf, gain_final, seq, tiles["combine"],
                    final_norm)


PROMPT_TILES = dict(qkv=512, wo=256, pool=256, route=512, dispatch=256, bm=256, combine=128)
SAMPLE_TILES = dict(qkv=32, wo=32, pool=32, route=256, dispatch=256, bm=64, combine=32)


def _trunk(x, mods, pos0, past_k, past_v, past_pool, p, tiles):
    b, s, d = x.shape
    depth = len(mods)
    x2d = x.reshape(b * s, d)
    tabs = _rope_tables(pos0 + jnp.arange(s, dtype=jnp.int32))
    new_k, new_v, new_pool = [], [], []
    for i in range(depth):
        mod = mods[i]
        w_rt = p["w_router"][i]
        if i % N_MIXERS == 0:
            a = i // N_MIXERS
            lam_init = 0.8 - 0.6 * math.exp(-0.3 * i)
            lam_vecs = jnp.stack([p["lambda_q1"][a], p["lambda_k1"][a], p["lambda_q2"][a], p["lambda_k2"][a]])
            q, k, v = _qkv(x2d, mod, p["norm_mix"][i], p["w_qkv_bf"][a], tabs, s, tiles["qkv"])
            if past_k is None:
                o = _attn_prompt(q, k, v, lam_vecs, p["subln_gain"][a], b, s, lam_init)
            else:
                width = k.shape[1]
                o = _attn_sample(q, k, v, past_k[a].reshape(-1, width), past_v[a].reshape(-1, width),
                                 lam_vecs, p["subln_gain"][a], b, s, lam_init)
            new_k.append(k.reshape(b, s, width_heads(k), HEAD_DIM))
            new_v.append(v.reshape(b, s, width_heads(v) // 2, 2 * HEAD_DIM))
            x1, h2, lg = _wo_post(o, p["w_o_bf"][a], x2d, mod, p["norm_ffn"][i], w_rt, s, tiles["wo"])
        else:
            pi = i // N_MIXERS
            if past_pool is None:
                prev = jnp.zeros((b, POOL_MAX, d), F32)
            else:
                prev = jnp.concatenate([jnp.zeros((b, 1, d), F32), past_pool[pi]], axis=1)
            x1, h2, lg, npool = _pool_post(x2d, prev, mod, p["norm_mix"][i], p["w_pool_bf"][pi], p["pool_scale"][pi],
                                           p["norm_ffn"][i], w_rt, b, s, pos0, tiles["pool"])
            new_pool.append(npool[:, 1:, :])
        x2d = _moe(x1, h2, lg, mod, p["router_bias"][i], p["w_exp_gu"], p["w_exp_down"], i,
                   p["w_sh_gu_bf"][i], p["w_sh_down_bf"][i], p["norm_final"], s, tiles,
                   final_norm=(i == depth - 1))
    return x2d.reshape(b, s, d), jnp.stack(new_k), jnp.stack(new_v), jnp.stack(new_pool)


def width_heads(a):
    return a.shape[1] // HEAD_DIM


def kernel(x_prompt, x_sample, c_prompt, c_sample, cache_k, cache_v, state_pool, w_ada, b_ada, norm_mix, norm_ffn,
           norm_final, w_qkv, w_o, lambda_q1, lambda_k1, lambda_q2, lambda_k2, subln_gain, w_pool, pool_scale,
           w_router, router_bias, w_exp_gu, w_exp_down, w_sh_gu, w_sh_down):
    depth, d = norm_mix.shape
    nb = c_prompt.shape[0]
    mod_all = _ada(jnp.concatenate([c_prompt, c_sample], axis=0), w_ada, b_ada)
    mod_all = mod_all.reshape(depth, -1, N_ADA, d)
    mods_p = [mod_all[i, :nb] for i in range(depth)]
    mods_s = [mod_all[i, nb:] for i in range(depth)]
    p = dict(norm_mix=norm_mix, norm_ffn=norm_ffn, norm_final=norm_final,
             lambda_q1=lambda_q1, lambda_k1=lambda_k1, lambda_q2=lambda_q2, lambda_k2=lambda_k2,
             subln_gain=subln_gain, pool_scale=pool_scale, w_router=w_router, router_bias=router_bias,
             w_qkv_bf=w_qkv.astype(BF16), w_o_bf=w_o.astype(BF16), w_pool_bf=w_pool.astype(BF16),
             w_exp_gu=w_exp_gu, w_exp_down=w_exp_down,
             w_sh_gu_bf=w_sh_gu.astype(BF16), w_sh_down_bf=w_sh_down.astype(BF16))
    y_p, k_p, v_p, pool_p = _trunk(x_prompt, mods_p, 0, None, None, None, p, PROMPT_TILES)
    y_s, k_s, v_s, pool_s = _trunk(x_sample, mods_s, cache_k.shape[2], cache_k, cache_v, state_pool, p, SAMPLE_TILES)
    return (y_p, y_s, k_p, v_p, pool_p, k_s, v_s, pool_s)
```

```python
import functools
import math

import jax
import jax.numpy as jnp
from jax import lax
from jax.experimental import pallas as pl
from jax.experimental.pallas import tpu as pltpu

CHUNK = 64
HEAD_DIM = 128
ROT_DIM = HEAD_DIM // 4
ROPE_THETA = 500000.0
POOL_WINDOWS = (2, 4, 8, 16)
POOL_MAX = 16
TOP_K = 8
ROUTED_SCALE = 2.5
NORM_EPS = 1e-6
N_ADA = 6
N_MIXERS = 2

LANES = 128
LOG2_E = math.log2(math.e)
MIB = 1024 * 1024

F32 = jnp.float32
BF16 = jnp.bfloat16
NT_DIMS = (((1,), (1,)), ((), ()))


def _cparams(sem, vmem_mib):
    return pltpu.CompilerParams(dimension_semantics=sem, vmem_limit_bytes=vmem_mib * MIB)


def _rms(x, gain):
    return x * lax.rsqrt(jnp.mean(x * x, axis=-1, keepdims=True) + NORM_EPS) * gain


def _norm_mod(x, gain, scale, shift):
    return _rms(x, gain) * (1.0 + scale) + shift


def _swiglu(x_bf, w_gu, w_down, ff):
    au = jnp.dot(x_bf, w_gu, preferred_element_type=F32)
    act = (jax.nn.silu(au[:, :ff]) * au[:, ff:]).astype(BF16)
    return jnp.dot(act, w_down, preferred_element_type=F32)


def _ada_kernel(c_ref, w_ref, b_ref, o_ref):
    c_act = jax.nn.silu(c_ref[...]).astype(BF16)
    o_ref[0] = jnp.dot(c_act, w_ref[0].astype(BF16), preferred_element_type=F32) + b_ref[0]


def _ada(c_all, w_ada, b_ada, tn=1024):
    depth, d, n = w_ada.shape
    r = c_all.shape[0]
    return pl.pallas_call(
        _ada_kernel,
        grid=(depth, n // tn),
        in_specs=[pl.BlockSpec((r, d), lambda i, j: (0, 0)),
                  pl.BlockSpec((1, d, tn), lambda i, j: (i, 0, j)),
                  pl.BlockSpec((1, 1, tn), lambda i, j: (i, 0, j))],
        out_specs=pl.BlockSpec((1, r, tn), lambda i, j: (i, 0, j)),
        out_shape=jax.ShapeDtypeStruct((depth, r, n), F32),
        compiler_params=_cparams(("arbitrary", "arbitrary"), 40),
        name="ada_mod",
    )(c_all, w_ada, b_ada.reshape(depth, 1, n))


def _qkv_kernel(x_ref, mod_ref, g_ref, w_hbm, cos_ref, sa_ref, sb_ref, q_ref, k_ref, v_ref, w_vmem, h_scr, sem, *, tn):
    @pl.when(pl.program_id(0) == 0)
    def _():
        load = pltpu.make_async_copy(w_hbm, w_vmem, sem)
        load.start()
        load.wait()

    h_scr[...] = _norm_mod(x_ref[...], g_ref[...], mod_ref[0, 1:2, :], mod_ref[0, 0:1, :]).astype(BF16)
    width = q_ref.shape[1]
    cos, sa, sb = cos_ref[...], sa_ref[...], sb_ref[...]

    def rope_store(acc, dst, off):
        for c in range(tn // HEAD_DIM):
            blk = acc[:, c * HEAD_DIM:(c + 1) * HEAD_DIM]
            rot = blk * cos + pltpu.roll(blk, HEAD_DIM - ROT_DIM // 2, 1) * sa + pltpu.roll(blk, ROT_DIM // 2, 1) * sb
            dst[:, off + c * HEAD_DIM:off + (c + 1) * HEAD_DIM] = rot.astype(dst.dtype)

    for j in range(3 * width // tn):
        acc = jnp.dot(h_scr[...], w_vmem[:, j * tn:(j + 1) * tn], preferred_element_type=F32)
        part, off = divmod(j * tn, width)
        if part == 0:
            rope_store(acc, q_ref, off)
        elif part == 1:
            rope_store(acc, k_ref, off)
        else:
            v_ref[:, off:off + tn] = acc


def _qkv(x2d, mod, gain, w_bf, tabs, seq, tm, tn=512):
    t, d = x2d.shape
    w3 = w_bf.shape[1]
    width = w3 // 3
    spt = seq // tm
    cos_t, sa_t, sb_t = tabs
    tab_spec = pl.BlockSpec((tm, HEAD_DIM), lambda i: (i % spt, 0))
    out_spec = pl.BlockSpec((tm, width), lambda i: (i, 0))
    return pl.pallas_call(
        functools.partial(_qkv_kernel, tn=tn),
        grid=(t // tm,),
        in_specs=[pl.BlockSpec((tm, d), lambda i: (i, 0)),
                  pl.BlockSpec((1, N_ADA, d), lambda i: (i // spt, 0, 0)),
                  pl.BlockSpec((1, d), lambda i: (0, 0)),
                  pl.BlockSpec(memory_space=pl.ANY),
                  tab_spec, tab_spec, tab_spec],
        out_specs=[out_spec, out_spec, out_spec],
        out_shape=[jax.ShapeDtypeStruct((t, width), BF16),
                   jax.ShapeDtypeStruct((t, width), F32),
                   jax.ShapeDtypeStruct((t, width), F32)],
        scratch_shapes=[pltpu.VMEM((d, w3), BF16), pltpu.VMEM((tm, d), BF16), pltpu.SemaphoreType.DMA(())],
        compiler_params=_cparams(("arbitrary",), 56),
        name="qkv_rope",
    )(x2d, mod, gain.reshape(1, d), w_bf, cos_t, sa_t, sb_t)


def _rope_tables(pos):
    half = ROT_DIM // 2
    inv_freq = ROPE_THETA ** (-jnp.arange(half, dtype=F32) / half)
    ang = pos.astype(F32)[:, None] * inv_freq[None, :]
    cos, sin = jnp.cos(ang), jnp.sin(ang)
    s = pos.shape[0]
    rest = HEAD_DIM - ROT_DIM
    cos_t = jnp.concatenate([cos, cos, jnp.ones((s, rest), F32)], axis=1)
    sa_t = jnp.concatenate([-sin, jnp.zeros((s, half + rest), F32)], axis=1)
    sb_t = jnp.concatenate([jnp.zeros((s, half), F32), sin, jnp.zeros((s, rest), F32)], axis=1)
    return cos_t, sa_t, sb_t


def _lam_value(lam_ref, lam_init):
    l1 = jnp.sum(lam_ref[0:1, :] * lam_ref[1:2, :], axis=1, keepdims=True)
    l2 = jnp.sum(lam_ref[2:3, :] * lam_ref[3:4, :], axis=1, keepdims=True)
    return jnp.exp(l1) - jnp.exp(l2) + lam_init


def _softmax_parts(parts, scale):
    mx = None
    for s in parts:
        m = jnp.max(s, axis=1, keepdims=True)
        mx = m if mx is None else jnp.maximum(mx, m)
    ps = [jnp.exp2((s - mx) * (scale * LOG2_E)) for s in parts]
    l = None
    for p in ps:
        r = jnp.sum(p, axis=1, keepdims=True)
        l = r if l is None else l + r
    return ps, l


def _subln(o, sub_ref, lam_init):
    return _rms(o, sub_ref[...]) * (1.0 - lam_init)


def _attn_prompt_kernel(q_ref, k_ref, v_ref, lam_ref, sub_ref, o_ref, kb, vb, *, tq, lam_init):
    seq = q_ref.shape[0]
    hd = HEAD_DIM
    scale = hd ** -0.5
    kb[...] = k_ref[...].astype(BF16)
    vb[...] = v_ref[...].astype(BF16)
    lam = _lam_value(lam_ref, lam_init)
    row = lax.broadcasted_iota(jnp.int32, (tq, tq), 0) // CHUNK
    col = lax.broadcasted_iota(jnp.int32, (tq, tq), 1) // CHUNK
    diag_mask = col <= row
    for qb in range(seq // tq):
        st, end = qb * tq, (qb + 1) * tq
        probs, coef = [], []
        for m in range(2):
            qm = q_ref[st:end, m * hd:(m + 1) * hd]
            km = kb[0:end, m * hd:(m + 1) * hd]
            s = lax.dot_general(qm, km, NT_DIMS, preferred_element_type=F32)
            parts = [s[:, :st]] if st > 0 else []
            parts.append(jnp.where(diag_mask, s[:, st:], -1e30))
            ps, l = _softmax_parts(parts, scale)
            probs.append(ps)
            coef.append((1.0 / l) if m == 0 else (lam / l))
        o = None
        bounds = ([(0, st)] if st > 0 else []) + [(st, end)]
        for idx, (lo, hi) in enumerate(bounds):
            a = (probs[0][idx] * coef[0] - probs[1][idx] * coef[1]).astype(BF16)
            part = jnp.dot(a, vb[lo:hi, :], preferred_element_type=F32)
            o = part if o is None else o + part
        o_ref[st:end, :] = _subln(o, sub_ref, lam_init).astype(o_ref.dtype)


def _attn_prompt(q, k, v, lam_vecs, subln, batch, seq, lam_init, tq=256):
    t, width = q.shape
    hw = 2 * HEAD_DIM
    spec = pl.BlockSpec((seq, hw), lambda b, h: (b, h))
    return pl.pallas_call(
        functools.partial(_attn_prompt_kernel, tq=tq, lam_init=lam_init),
        grid=(batch, width // hw),
        in_specs=[spec, spec, spec,
                  pl.BlockSpec((4, HEAD_DIM), lambda b, h: (0, 0)),
                  pl.BlockSpec((1, hw), lambda b, h: (0, 0))],
        out_specs=spec,
        out_shape=jax.ShapeDtypeStruct((t, width), BF16),
        scratch_shapes=[pltpu.VMEM((seq, hw), BF16), pltpu.VMEM((seq, hw), BF16)],
        compiler_params=_cparams(("arbitrary", "arbitrary"), 56),
        name="attn_prompt",
    )(q, k, v, lam_vecs, subln.reshape(1, hw))


def _attn_sample_kernel(q_ref, k_ref, v_ref, pk_ref, pv_ref, lam_ref, sub_ref, o_ref, *, lam_init):
    hd = HEAD_DIM
    scale = hd ** -0.5
    lam = _lam_value(lam_ref, lam_init)
    pk = pk_ref[...].astype(BF16)
    kn = k_ref[...].astype(BF16)
    probs, coef = [], []
    for m in range(2):
        qm = q_ref[:, m * hd:(m + 1) * hd]
        parts = [lax.dot_general(qm, pk[:, m * hd:(m + 1) * hd], NT_DIMS, preferred_element_type=F32),
                 lax.dot_general(qm, kn[:, m * hd:(m + 1) * hd], NT_DIMS, preferred_element_type=F32)]
        ps, l = _softmax_parts(parts, scale)
        probs.append(ps)
        coef.append((1.0 / l) if m == 0 else (lam / l))
    vals = [pv_ref[...].astype(BF16), v_ref[...].astype(BF16)]
    o = None
    for idx in range(2):
        a = (probs[0][idx] * coef[0] - probs[1][idx] * coef[1]).astype(BF16)
        part = jnp.dot(a, vals[idx], preferred_element_type=F32)
        o = part if o is None else o + part
    o_ref[...] = _subln(o, sub_ref, lam_init).astype(o_ref.dtype)


def _attn_sample(q, k, v, past_k, past_v, lam_vecs, subln, batch, seq, lam_init):
    t, width = q.shape
    hw = 2 * HEAD_DIM
    past = past_k.shape[0] // batch
    spec = pl.BlockSpec((seq, hw), lambda b, h: (b, h))
    pspec = pl.BlockSpec((past, hw), lambda b, h: (b, h))
    return pl.pallas_call(
        functools.partial(_attn_sample_kernel, lam_init=lam_init),
        grid=(batch, width // hw),
        in_specs=[spec, spec, spec, pspec, pspec,
                  pl.BlockSpec((4, HEAD_DIM), lambda b, h: (0, 0)),
                  pl.BlockSpec((1, hw), lambda b, h: (0, 0))],
        out_specs=spec,
        out_shape=jax.ShapeDtypeStruct((t, width), BF16),
        compiler_params=_cparams(("arbitrary", "arbitrary"), 48),
        name="attn_sample",
    )(q, k, v, past_k, past_v, lam_vecs, subln.reshape(1, hw))


def _post_mixer(x, y, mod_ref, gf_ref, wr_ref, x1_ref, h2_ref, lg_ref):
    x1 = x + mod_ref[0, 2:3, :] * y
    h2 = _norm_mod(x1, gf_ref[...], mod_ref[0, 4:5, :], mod_ref[0, 3:4, :])
    x1_ref[...] = x1
    h2_ref[...] = h2
    h_hi = h2.astype(BF16)
    h_lo = (h2 - h_hi.astype(F32)).astype(BF16)
    w = wr_ref[...]
    w_hi = w.astype(BF16)
    w_lo = (w - w_hi.astype(F32)).astype(BF16)
    lg_ref[...] = (jnp.dot(h_hi, w_hi, preferred_element_type=F32)
                   + (jnp.dot(h_hi, w_lo, preferred_element_type=F32)
                      + jnp.dot(h_lo, w_hi, preferred_element_type=F32)))


def _wo_kernel(o_ref, w_ref, x_ref, mod_ref, gf_ref, wr_ref, x1_ref, h2_ref, lg_ref):
    y = jnp.dot(o_ref[...], w_ref[...], preferred_element_type=F32)
    _post_mixer(x_ref[...], y, mod_ref, gf_ref, wr_ref, x1_ref, h2_ref, lg_ref)


def _wo_post(o, w_bf, x2d, mod, gain_ffn, w_router, seq, tm):
    t, d = x2d.shape
    width = o.shape[1]
    e = w_router.shape[1]
    spt = seq // tm
    row = pl.BlockSpec((tm, d), lambda i: (i, 0))
    x1, h2, lg = pl.pallas_call(
        _wo_kernel,
        grid=(t // tm,),
        in_specs=[pl.BlockSpec((tm, width), lambda i: (i, 0)),
                  pl.BlockSpec((width, d), lambda i: (0, 0)),
                  row,
                  pl.BlockSpec((1, N_ADA, d), lambda i: (i // spt, 0, 0)),
                  pl.BlockSpec((1, d), lambda i: (0, 0)),
                  pl.BlockSpec((d, e), lambda i: (0, 0))],
        out_specs=[row, row, pl.BlockSpec((tm, e), lambda i: (i, 0))],
        out_shape=[jax.ShapeDtypeStruct((t, d), F32), jax.ShapeDtypeStruct((t, d), F32),
                   jax.ShapeDtypeStruct((t, e), F32)],
        compiler_params=_cparams(("arbitrary",), 48),
        name="wo_post",
    )(o, w_bf, x2d, mod, gain_ffn.reshape(1, d), w_router)
    return x1, h2, lg.T


def _pool_kernel(x_ref, xh_ref, prev_ref, mod_ref, gm_ref, wp_ref, ps_ref, gf_ref, wr_ref,
                 x1_ref, h2_ref, lg_ref, np_ref, hext, ybuf, *, pos0, tm):
    s = pl.program_id(1)
    halo = POOL_MAX
    x = x_ref[...]
    sc1, sh1 = mod_ref[0, 1:2, :], mod_ref[0, 0:1, :]
    h = _norm_mod(x, gm_ref[...], sc1, sh1)
    h_halo = _norm_mod(xh_ref[...], gm_ref[...], sc1, sh1)
    hext[0:halo, :] = jnp.where(s == 0, prev_ref[0], h_halo)
    hext[halo:, :] = h
    pos = pos0 + s * tm + lax.broadcasted_iota(jnp.int32, (tm, 1), 0)
    gc = x.shape[1] // len(POOL_WINDOWS)
    for g, w in enumerate(POOL_WINDOWS):
        c0, c1 = g * gc, (g + 1) * gc
        tot = hext[halo:halo + tm, c0:c1]
        for k in range(1, w):
            tot = tot + hext[halo - k:halo - k + tm, c0:c1]
        cnt = jnp.minimum(w, pos + 1).astype(F32)
        dlt = tot / cnt - hext[halo:halo + tm, c0:c1]
        yg = jnp.dot(dlt.astype(BF16), wp_ref[g], preferred_element_type=F32)
        ybuf[:, c0:c1] = yg * ps_ref[:, c0:c1]
    _post_mixer(x, ybuf[...], mod_ref, gf_ref, wr_ref, x1_ref, h2_ref, lg_ref)

    @pl.when(s == pl.num_programs(1) - 1)
    def _():
        np_ref[0] = hext[tm:tm + halo, :]


def _pool_post(x2d, prev, mod, gain_mix, wp_bf, pool_scale, gain_ffn, w_router, batch, seq, pos0, tm):
    t, d = x2d.shape
    e = w_router.shape[1]
    spt = seq // tm
    hpt = tm // POOL_MAX
    ng, gc = wp_bf.shape[0], wp_bf.shape[1]
    row = pl.BlockSpec((tm, d), lambda b, s: (b * spt + s, 0))
    vec = pl.BlockSpec((1, d), lambda b, s: (0, 0))
    x1, h2, lg, new_pool = pl.pallas_call(
        functools.partial(_pool_kernel, pos0=pos0, tm=tm),
        grid=(batch, spt),
        in_specs=[row,
                  pl.BlockSpec((POOL_MAX, d), lambda b, s: (jnp.maximum((b * spt + s) * hpt - 1, 0), 0)),
                  pl.BlockSpec((1, POOL_MAX, d), lambda b, s: (b, 0, 0)),
                  pl.BlockSpec((1, N_ADA, d), lambda b, s: (b, 0, 0)),
                  vec,
                  pl.BlockSpec((ng, gc, gc), lambda b, s: (0, 0, 0)),
                  vec, vec,
                  pl.BlockSpec((d, e), lambda b, s: (0, 0))],
        out_specs=[row, row, pl.BlockSpec((tm, e), lambda b, s: (b * spt + s, 0)),
                   pl.BlockSpec((1, POOL_MAX, d), lambda b, s: (b, 0, 0))],
        out_shape=[jax.ShapeDtypeStruct((t, d), F32), jax.ShapeDtypeStruct((t, d), F32),
                   jax.ShapeDtypeStruct((t, e), F32), jax.ShapeDtypeStruct((batch, POOL_MAX, d), F32)],
        scratch_shapes=[pltpu.VMEM((POOL_MAX + tm, d), F32), pltpu.VMEM((tm, d), F32)],
        compiler_params=_cparams(("arbitrary", "arbitrary"), 48),
        name="pool_post",
    )(x2d, x2d, prev, mod, gain_mix.reshape(1, d), wp_bf, pool_scale.reshape(1, d), gain_ffn.reshape(1, d), w_router)
    return x1, h2, lg.T, new_pool


def _route_kernel(lg_ref, bias_ref, idx_ref, gate_ref, rank_ref, cnt_ref, carry):
    i = pl.program_id(0)

    @pl.when(i == 0)
    def _():
        carry[...] = jnp.zeros_like(carry)

    score = jax.nn.sigmoid(lg_ref[...])
    n_e, tm = score.shape
    val = score + bias_ref[...]
    e_iota = lax.broadcasted_iota(jnp.int32, (n_e, tm), 0)
    sel = jnp.zeros((n_e, tm), F32)
    idxs, picked = [], []
    for _ in range(TOP_K):
        best = jnp.max(val, axis=0, keepdims=True)
        ix = jnp.min(jnp.where(val == best, e_iota, n_e), axis=0, keepdims=True)
        hit = e_iota == ix
        picked.append(jnp.sum(jnp.where(hit, score, 0.0), axis=0, keepdims=True))
        idxs.append(ix)
        val = jnp.where(hit, -jnp.inf, val)
        sel = jnp.where(hit, 1.0, sel)
    tri = (lax.broadcasted_iota(jnp.int32, (tm, tm), 0) <= lax.broadcasted_iota(jnp.int32, (tm, tm), 1)).astype(BF16)
    incl = jnp.dot(sel.astype(BF16), tri, preferred_element_type=F32)
    excl = incl - sel + carry[:, 0:1]
    total = picked[0]
    for p in picked[1:]:
        total = total + p
    for r in range(TOP_K):
        hit = e_iota == idxs[r]
        idx_ref[r:r + 1, :] = idxs[r]
        gate_ref[r:r + 1, :] = picked[r] / total * ROUTED_SCALE
        rank_ref[r:r + 1, :] = jnp.sum(jnp.where(hit, excl, 0.0), axis=0, keepdims=True).astype(jnp.int32)
    carry[...] = carry[...] + jnp.sum(sel, axis=1, keepdims=True)
    cnt_ref[...] = carry[...]


def _route(logits_t, bias, tm):
    e, t = logits_t.shape
    out = pl.BlockSpec((TOP_K, tm), lambda i: (0, i))
    return pl.pallas_call(
        _route_kernel,
        grid=(t // tm,),
        in_specs=[pl.BlockSpec((e, tm), lambda i: (0, i)), pl.BlockSpec((e, 1), lambda i: (0, 0))],
        out_specs=[out, out, out, pl.BlockSpec((e, LANES), lambda i: (0, 0))],
        out_shape=[jax.ShapeDtypeStruct((TOP_K, t), jnp.int32), jax.ShapeDtypeStruct((TOP_K, t), F32),
                   jax.ShapeDtypeStruct((TOP_K, t), jnp.int32), jax.ShapeDtypeStruct((e, LANES), F32)],
        scratch_shapes=[pltpu.VMEM((e, LANES), F32)],
        compiler_params=_cparams(("arbitrary",), 32),
        name="route_topk",
    )(logits_t, bias.reshape(e, 1))


DISPATCH_CHUNK = 32


def _row_copy(src, src_row, dst, dst_row, sem):
    return pltpu.make_async_copy(src.at[pl.ds(src_row, 1), :], dst.at[pl.ds(dst_row, 1), :], sem)


def _dest_kernel(ps_ref, idx_ref, rank_ref, dest_ref):
    idx = idx_ref[...]
    dest = rank_ref[...]
    for e in range(ps_ref.shape[0]):
        dest = dest + jnp.where(idx == e, ps_ref[e], 0)
    dest_ref[...] = dest


def _dest_rows(idx_t, rank_t, pad_start, tm):
    k, t = idx_t.shape
    blk = pl.BlockSpec((k, tm), lambda i, ps: (0, i))
    return pl.pallas_call(
        _dest_kernel,
        grid_spec=pltpu.PrefetchScalarGridSpec(num_scalar_prefetch=1, grid=(t // tm,), in_specs=[blk, blk],
                                               out_specs=blk),
        out_shape=jax.ShapeDtypeStruct((k, t), jnp.int32),
        compiler_params=_cparams(("arbitrary",), 32),
        name="moe_dest",
    )(pad_start, idx_t, rank_t)


def _pack_bf16_pairs(x):
    half = x.shape[1] // 2
    lo = lax.bitcast_convert_type(x[:, :half].astype(BF16).astype(F32), jnp.uint32)
    hi = lax.bitcast_convert_type(x[:, half:].astype(BF16).astype(F32), jnp.uint32)
    return (lo >> 16) | (hi & jnp.uint32(0xFFFF0000))


def _unpack_bf16_pairs(w):
    lo = lax.bitcast_convert_type(w << 16, F32).astype(BF16)
    hi = lax.bitcast_convert_type(w & jnp.uint32(0xFFFF0000), F32).astype(BF16)
    return lo, hi


def _dispatch_kernel(dest_ref, h_ref, xs_ref, packed, sems, *, td):
    i = pl.program_id(0)
    slot = i % 2
    n_chunks = td // DISPATCH_CHUNK
    src = packed.at[slot]
    src[...] = _pack_bf16_pairs(h_ref[...])

    for c in range(n_chunks):
        for t in range(c * DISPATCH_CHUNK, (c + 1) * DISPATCH_CHUNK):
            for r in range(TOP_K):
                _row_copy(src, t, xs_ref, dest_ref[r, t], sems.at[slot, c]).start()

    def wait_step(s):
        for c in range(n_chunks):
            for _ in range(DISPATCH_CHUNK * TOP_K):
                _row_copy(packed.at[s], 0, xs_ref, 0, sems.at[s, c]).wait()

    @pl.when(i > 0)
    def _():
        wait_step(1 - slot)

    @pl.when(i == pl.num_programs(0) - 1)
    def _():
        wait_step(slot)


def _dispatch(h2, dest_t, n_buf, td):
    t, d = h2.shape
    return pl.pallas_call(
        functools.partial(_dispatch_kernel, td=td),
        grid=(t // td,),
        in_specs=[pl.BlockSpec((TOP_K, td), lambda i: (0, i), memory_space=pltpu.SMEM),
                  pl.BlockSpec((td, d), lambda i: (i, 0))],
        out_specs=pl.BlockSpec(memory_space=pl.ANY),
        out_shape=jax.ShapeDtypeStruct((n_buf, d // 2), jnp.uint32),
        scratch_shapes=[pltpu.VMEM((2, td, d // 2), jnp.uint32),
                        pltpu.SemaphoreType.DMA((2, td // DISPATCH_CHUNK))],
        compiler_params=_cparams(("arbitrary",), 32),
        name="moe_dispatch",
    )(dest_t, h2)


def _expert_kernel(be_ref, bi_ref, nv_ref, xs_ref, wgu_ref, wdn_ref, ys_ref, wgu_bf, wdn_bf, *, ff):
    i = pl.program_id(0)
    nv = nv_ref[i]

    @pl.when((i == 0) | (be_ref[i] != be_ref[jnp.maximum(i - 1, 0)]))
    def _():
        wgu_bf[...] = wgu_ref[0, 0].astype(BF16)
        wdn_bf[...] = wdn_ref[0, 0].astype(BF16)

    @pl.when(nv > 0)
    def _():
        w = xs_ref[...]
        rows = lax.broadcasted_iota(jnp.int32, (w.shape[0], 1), 0)
        w = jnp.where(rows < nv, w, jnp.uint32(0))
        x_lo, x_hi = _unpack_bf16_pairs(w)
        half = w.shape[1]
        au = (jnp.dot(x_lo, wgu_bf[0:half, :], preferred_element_type=F32)
              + jnp.dot(x_hi, wgu_bf[half:, :], preferred_element_type=F32))
        act = (jax.nn.silu(au[:, :ff]) * au[:, ff:]).astype(BF16)
        ys_ref[...] = jnp.dot(act, wdn_bf[...], preferred_element_type=F32)


def _experts(xs, w_gu, w_down, layer, block_e, block_i, block_nv, bm):
    n_buf = xs.shape[0]
    d, ff = w_gu.shape[2], w_down.shape[2]
    row_in = pl.BlockSpec((bm, d // 2), lambda i, be, bi, nv: (bi[i], 0))
    row = pl.BlockSpec((bm, d), lambda i, be, bi, nv: (bi[i], 0))
    return pl.pallas_call(
        functools.partial(_expert_kernel, ff=ff),
        grid_spec=pltpu.PrefetchScalarGridSpec(
            num_scalar_prefetch=3,
            grid=(n_buf // bm,),
            in_specs=[row_in,
                      pl.BlockSpec((1, 1, d, 2 * ff), lambda i, be, bi, nv: (layer, be[i], 0, 0)),
                      pl.BlockSpec((1, 1, ff, d), lambda i, be, bi, nv: (layer, be[i], 0, 0))],
            out_specs=row,
            scratch_shapes=[pltpu.VMEM((d, 2 * ff), BF16), pltpu.VMEM((ff, d), BF16)]),
        out_shape=jax.ShapeDtypeStruct((n_buf, d), F32),
        compiler_params=_cparams(("arbitrary",), 56),
        name="moe_experts",
    )(block_e, block_i, block_nv, xs, w_gu, w_down)


def _combine_kernel(dest_ref, dnext_ref, ys_ref, gate_ref, h2_ref, x1_ref, mod_ref, wsg_ref, wsd_ref, gfin_ref,
                    out_ref, buf, sems, *, tc, ff, final_norm, whole_index):
    i = pl.program_id(0)
    n = pl.num_programs(0)
    slot = i % 2

    def start_tile(table, tok0, s):
        for t in range(tc):
            for r in range(TOP_K):
                pltpu.make_async_copy(ys_ref.at[pl.ds(table[r, tok0 + t], 1), :], buf.at[s, r, pl.ds(t, 1), :],
                                      sems.at[s, r]).start()

    @pl.when(i == 0)
    def _():
        start_tile(dest_ref, 0, 0)

    @pl.when(i + 1 < n)
    def _():
        start_tile(dnext_ref, (i + 1) * tc if whole_index else 0, 1 - slot)

    acc = _swiglu(h2_ref[...].astype(BF16), wsg_ref[...], wsd_ref[...], ff)

    for r in range(TOP_K):
        for _ in range(tc):
            pltpu.make_async_copy(ys_ref.at[pl.ds(0, 1), :], buf.at[slot, r, pl.ds(0, 1), :], sems.at[slot, r]).wait()

    gate = gate_ref[...]
    routed = gate[:, 0:1] * buf[slot, 0]
    for r in range(1, TOP_K):
        routed = routed + gate[:, r:r + 1] * buf[slot, r]
    x2 = x1_ref[...] + mod_ref[0, 5:6, :] * (routed + acc)
    if final_norm:
        x2 = _rms(x2, gfin_ref[...])
    out_ref[...] = x2


def _combine(ys, dest_t, gates, h2, x1, mod, ws_gu_bf, ws_dn_bf, gain_final, seq, tc, final_norm):
    t, d = x1.shape
    ff = ws_dn_bf.shape[0]
    spt = seq // tc
    n = t // tc
    whole_index = tc % LANES != 0
    if whole_index:
        smem = smem_next = pl.BlockSpec((TOP_K, t), lambda i: (0, 0), memory_space=pltpu.SMEM)
    else:
        smem = pl.BlockSpec((TOP_K, tc), lambda i: (0, i), memory_space=pltpu.SMEM)
        smem_next = pl.BlockSpec((TOP_K, tc), lambda i: (0, jnp.minimum(i + 1, n - 1)), memory_space=pltpu.SMEM)
    row = pl.BlockSpec((tc, d), lambda i: (i, 0))
    return pl.pallas_call(
        functools.partial(_combine_kernel, tc=tc, ff=ff, final_norm=final_norm, whole_index=whole_index),
        grid=(n,),
        in_specs=[smem, smem_next,
                  pl.BlockSpec(memory_space=pl.ANY),
                  pl.BlockSpec((tc, TOP_K), lambda i: (i, 0)),
                  row, row,
                  pl.BlockSpec((1, N_ADA, d), lambda i: (i // spt, 0, 0)),
                  pl.BlockSpec((d, 2 * ff), lambda i: (0, 0)),
                  pl.BlockSpec((ff, d), lambda i: (0, 0)),
                  pl.BlockSpec((1, d), lambda i: (0, 0))],
        out_specs=row,
        out_shape=jax.ShapeDtypeStruct((t, d), F32),
        scratch_shapes=[pltpu.VMEM((2, TOP_K, tc, d), F32), pltpu.SemaphoreType.DMA((2, TOP_K))],
        compiler_params=_cparams(("arbitrary",), 56),
        name="moe_combine",
    )(dest_t, dest_t, ys, gates, h2, x1, mod, ws_gu_bf, ws_dn_bf, gain_final.reshape(1, d))


def _block_tables(counts, bm, n_blocks):
    n_e = counts.shape[0]
    padded = (counts + bm - 1) // bm * bm
    pad_end = jnp.cumsum(padded)
    pad_start = pad_end - padded
    used = pad_end[-1] // bm
    blk = jnp.arange(n_blocks, dtype=jnp.int32)
    src = jnp.minimum(blk, jnp.maximum(used - 1, 0))
    owner = jnp.sum((pad_end[None, :] <= (src * bm)[:, None]).astype(jnp.int32), axis=1)
    block_e = jnp.minimum(owner, n_e - 1)
    valid = jnp.clip(counts[block_e] - (src * bm - pad_start[block_e]), 0, bm)
    block_nv = jnp.where(blk < used, valid, 0).astype(jnp.int32)
    return pad_start.astype(jnp.int32), block_e.astype(jnp.int32), src.astype(jnp.int32), block_nv


def _moe(x1, h2, logits_t, mod, bias, w_exp_gu, w_exp_down, layer, ws_gu_bf, ws_dn_bf, gain_final, seq, tiles,
         final_norm):
    t, d = x1.shape
    n_e = logits_t.shape[0]
    bm = tiles["bm"]
    idx_t, gate_t, rank_t, cnt = _route(logits_t, bias, tiles["route"])
    counts = cnt[:, 0].astype(jnp.int32)
    n_blocks = -(-(t * TOP_K + n_e * (bm - 1)) // bm)
    pad_start, block_e, block_i, block_nv = _block_tables(counts, bm, n_blocks)
    dest_t = _dest_rows(idx_t, rank_t, pad_start, tiles["route"])
    xs = _dispatch(h2, dest_t, n_blocks * bm, tiles["dispatch"])
    ys = _experts(xs, w_exp_gu, w_exp_down, layer, block_e, block_i, block_nv, bm)
    return _combine(ys, dest_t, gate_t.T, h2, x1, mod, ws_gu_bf, ws_dn_bf, gain_final, seq, tiles["combine"],
                    final_norm)


PROMPT_TILES = dict(qkv=256, wo=256, pool=256, route=512, dispatch=256, bm=512, combine=128)
SAMPLE_TILES = dict(qkv=32, wo=32, pool=32, route=256, dispatch=256, bm=64, combine=32)


def _trunk(x, mods, pos0, past_k, past_v, past_pool, p, tiles):
    b, s, d = x.shape
    depth = len(mods)
    x2d = x.reshape(b * s, d)
    tabs = _rope_tables(pos0 + jnp.arange(s, dtype=jnp.int32))
    new_k, new_v, new_pool = [], [], []
    for i in range(depth):
        mod = mods[i]
        w_rt = p["w_router"][i]
        if i % N_MIXERS == 0:
            a = i // N_MIXERS
            lam_init = 0.8 - 0.6 * math.exp(-0.3 * i)
            lam_vecs = jnp.stack([p["lambda_q1"][a], p["lambda_k1"][a], p["lambda_q2"][a], p["lambda_k2"][a]])
            q, k, v = _qkv(x2d, mod, p["norm_mix"][i], p["w_qkv_bf"][a], tabs, s, tiles["qkv"])
            if past_k is None:
                o = _attn_prompt(q, k, v, lam_vecs, p["subln_gain"][a], b, s, lam_init)
            else:
                width = k.shape[1]
                o = _attn_sample(q, k, v, past_k[a].reshape(-1, width), past_v[a].reshape(-1, width),
                                 lam_vecs, p["subln_gain"][a], b, s, lam_init)
            new_k.append(k.reshape(b, s, width_heads(k), HEAD_DIM))
            new_v.append(v.reshape(b, s, width_heads(v) // 2, 2 * HEAD_DIM))
            x1, h2, lg = _wo_post(o, p["w_o_bf"][a], x2d, mod, p["norm_ffn"][i], w_rt, s, tiles["wo"])
        else:
            pi = i // N_MIXERS
            if past_pool is None:
                prev = jnp.zeros((b, POOL_MAX, d), F32)
            else:
                prev = jnp.concatenate([jnp.zeros((b, 1, d), F32), past_pool[pi]], axis=1)
            x1, h2, lg, npool = _pool_post(x2d, prev, mod, p["norm_mix"][i], p["w_pool_bf"][pi], p["pool_scale"][pi],
                                           p["norm_ffn"][i], w_rt, b, s, pos0, tiles["pool"])
            new_pool.append(npool[:, 1:, :])
        x2d = _moe(x1, h2, lg, mod, p["router_bias"][i], p["w_exp_gu"], p["w_exp_down"], i,
                   p["w_sh_gu_bf"][i], p["w_sh_down_bf"][i], p["norm_final"], s, tiles,
                   final_norm=(i == depth - 1))
    return x2d.reshape(b, s, d), jnp.stack(new_k), jnp.stack(new_v), jnp.stack(new_pool)


def width_heads(a):
    return a.shape[1] // HEAD_DIM


def kernel(x_prompt, x_sample, c_prompt, c_sample, cache_k, cache_v, state_pool, w_ada, b_ada, norm_mix, norm_ffn,
           norm_final, w_qkv, w_o, lambda_q1, lambda_k1, lambda_q2, lambda_k2, subln_gain, w_pool, pool_scale,
           w_router, router_bias, w_exp_gu, w_exp_down, w_sh_gu, w_sh_down):
    depth, d = norm_mix.shape
    nb = c_prompt.shape[0]
    mod_all = _ada(jnp.concatenate([c_prompt, c_sample], axis=0), w_ada, b_ada)
    mod_all = mod_all.reshape(depth, -1, N_ADA, d)
    mods_p = [mod_all[i, :nb] for i in range(depth)]
    mods_s = [mod_all[i, nb:] for i in range(depth)]
    p = dict(norm_mix=norm_mix, norm_ffn=norm_ffn, norm_final=norm_final,
             lambda_q1=lambda_q1, lambda_k1=lambda_k1, lambda_q2=lambda_q2, lambda_k2=lambda_k2,
             subln_gain=subln_gain, pool_scale=pool_scale, w_router=w_router, router_bias=router_bias,
             w_qkv_bf=w_qkv.astype(BF16), w_o_bf=w_o.astype(BF16), w_pool_bf=w_pool.astype(BF16),
             w_exp_gu=w_exp_gu, w_exp_down=w_exp_down,
             w_sh_gu_bf=w_sh_gu.astype(BF16), w_sh_down_bf=w_sh_down.astype(BF16))
    y_p, k_p, v_p, pool_p = _trunk(x_prompt, mods_p, 0, None, None, None, p, PROMPT_TILES)
    y_s, k_s, v_s, pool_s = _trunk(x_sample, mods_s, cache_k.shape[2], cache_k, cache_v, state_pool, p, SAMPLE_TILES)
    return (y_p, y_s, k_p, v_p, pool_p, k_s, v_s, pool_s)
```

```python
import functools
import math

import jax
import jax.numpy as jnp
from jax import lax
from jax.experimental import pallas as pl
from jax.experimental.pallas import tpu as pltpu

CHUNK = 64
HEAD_DIM = 128
ROT_DIM = HEAD_DIM // 4
ROPE_THETA = 500000.0
POOL_WINDOWS = (2, 4, 8, 16)
POOL_MAX = 16
TOP_K = 8
ROUTED_SCALE = 2.5
NORM_EPS = 1e-6
N_ADA = 6
N_MIXERS = 2

LANES = 128
LOG2_E = math.log2(math.e)
MIB = 1024 * 1024

F32 = jnp.float32
BF16 = jnp.bfloat16
NT_DIMS = (((1,), (1,)), ((), ()))


def _cparams(sem, vmem_mib):
    return pltpu.CompilerParams(dimension_semantics=sem, vmem_limit_bytes=vmem_mib * MIB)


def _rms(x, gain):
    return x * lax.rsqrt(jnp.mean(x * x, axis=-1, keepdims=True) + NORM_EPS) * gain


def _norm_mod(x, gain, scale, shift):
    return _rms(x, gain) * (1.0 + scale) + shift


def _swiglu(x_bf, w_gu, w_down, ff):
    au = jnp.dot(x_bf, w_gu, preferred_element_type=F32)
    act = (jax.nn.silu(au[:, :ff]) * au[:, ff:]).astype(BF16)
    return jnp.dot(act, w_down, preferred_element_type=F32)


def _ada_kernel(c_ref, w_ref, b_ref, o_ref):
    c_act = jax.nn.silu(c_ref[...]).astype(BF16)
    o_ref[0] = jnp.dot(c_act, w_ref[0].astype(BF16), preferred_element_type=F32) + b_ref[0]


def _ada(c_all, w_ada, b_ada, tn=1024):
    depth, d, n = w_ada.shape
    r = c_all.shape[0]
    return pl.pallas_call(
        _ada_kernel,
        grid=(depth, n // tn),
        in_specs=[pl.BlockSpec((r, d), lambda i, j: (0, 0)),
                  pl.BlockSpec((1, d, tn), lambda i, j: (i, 0, j)),
                  pl.BlockSpec((1, 1, tn), lambda i, j: (i, 0, j))],
        out_specs=pl.BlockSpec((1, r, tn), lambda i, j: (i, 0, j)),
        out_shape=jax.ShapeDtypeStruct((depth, r, n), F32),
        compiler_params=_cparams(("arbitrary", "arbitrary"), 40),
        name="ada_mod",
    )(c_all, w_ada, b_ada.reshape(depth, 1, n))


def _qkv_kernel(x_ref, mod_ref, g_ref, w_hbm, cos_ref, sa_ref, sb_ref, q_ref, k_ref, v_ref, w_vmem, h_scr, sem, *, tn):
    @pl.when(pl.program_id(0) == 0)
    def _():
        load = pltpu.make_async_copy(w_hbm, w_vmem, sem)
        load.start()
        load.wait()

    h_scr[...] = _norm_mod(x_ref[...], g_ref[...], mod_ref[0, 1:2, :], mod_ref[0, 0:1, :]).astype(BF16)
    width = q_ref.shape[1]
    cos, sa, sb = cos_ref[...], sa_ref[...], sb_ref[...]

    def rope_store(acc, dst, off):
        for c in range(tn // HEAD_DIM):
            blk = acc[:, c * HEAD_DIM:(c + 1) * HEAD_DIM]
            rot = blk * cos + pltpu.roll(blk, HEAD_DIM - ROT_DIM // 2, 1) * sa + pltpu.roll(blk, ROT_DIM // 2, 1) * sb
            dst[:, off + c * HEAD_DIM:off + (c + 1) * HEAD_DIM] = rot.astype(dst.dtype)

    for j in range(3 * width // tn):
        acc = jnp.dot(h_scr[...], w_vmem[:, j * tn:(j + 1) * tn], preferred_element_type=F32)
        part, off = divmod(j * tn, width)
        if part == 0:
            rope_store(acc, q_ref, off)
        elif part == 1:
            rope_store(acc, k_ref, off)
        else:
            v_ref[:, off:off + tn] = acc


def _qkv(x2d, mod, gain, w_bf, tabs, seq, tm, tn=512):
    t, d = x2d.shape
    w3 = w_bf.shape[1]
    width = w3 // 3
    spt = seq // tm
    cos_t, sa_t, sb_t = tabs
    tab_spec = pl.BlockSpec((tm, HEAD_DIM), lambda i: (i % spt, 0))
    out_spec = pl.BlockSpec((tm, width), lambda i: (i, 0))
    return pl.pallas_call(
        functools.partial(_qkv_kernel, tn=tn),
        grid=(t // tm,),
        in_specs=[pl.BlockSpec((tm, d), lambda i: (i, 0)),
                  pl.BlockSpec((1, N_ADA, d), lambda i: (i // spt, 0, 0)),
                  pl.BlockSpec((1, d), lambda i: (0, 0)),
                  pl.BlockSpec(memory_space=pl.ANY),
                  tab_spec, tab_spec, tab_spec],
        out_specs=[out_spec, out_spec, out_spec],
        out_shape=[jax.ShapeDtypeStruct((t, width), BF16),
                   jax.ShapeDtypeStruct((t, width), F32),
                   jax.ShapeDtypeStruct((t, width), F32)],
        scratch_shapes=[pltpu.VMEM((d, w3), BF16), pltpu.VMEM((tm, d), BF16), pltpu.SemaphoreType.DMA(())],
        compiler_params=_cparams(("arbitrary",), 56),
        name="qkv_rope",
    )(x2d, mod, gain.reshape(1, d), w_bf, cos_t, sa_t, sb_t)


def _rope_tables(pos):
    half = ROT_DIM // 2
    inv_freq = ROPE_THETA ** (-jnp.arange(half, dtype=F32) / half)
    ang = pos.astype(F32)[:, None] * inv_freq[None, :]
    cos, sin = jnp.cos(ang), jnp.sin(ang)
    s = pos.shape[0]
    rest = HEAD_DIM - ROT_DIM
    cos_t = jnp.concatenate([cos, cos, jnp.ones((s, rest), F32)], axis=1)
    sa_t = jnp.concatenate([-sin, jnp.zeros((s, half + rest), F32)], axis=1)
    sb_t = jnp.concatenate([jnp.zeros((s, half), F32), sin, jnp.zeros((s, rest), F32)], axis=1)
    return cos_t, sa_t, sb_t


def _lam_value(lam_ref, lam_init):
    l1 = jnp.sum(lam_ref[0:1, :] * lam_ref[1:2, :], axis=1, keepdims=True)
    l2 = jnp.sum(lam_ref[2:3, :] * lam_ref[3:4, :], axis=1, keepdims=True)
    return jnp.exp(l1) - jnp.exp(l2) + lam_init


def _softmax_parts(parts, scale):
    mx = None
    for s in parts:
        m = jnp.max(s, axis=1, keepdims=True)
        mx = m if mx is None else jnp.maximum(mx, m)
    ps = [jnp.exp2((s - mx) * (scale * LOG2_E)) for s in parts]
    l = None
    for p in ps:
        r = jnp.sum(p, axis=1, keepdims=True)
        l = r if l is None else l + r
    return ps, l


def _subln(o, sub_ref, lam_init):
    return _rms(o, sub_ref[...]) * (1.0 - lam_init)


def _attn_prompt_kernel(q_ref, k_ref, v_ref, lam_ref, sub_ref, o_ref, kb, vb, *, tq, lam_init):
    seq = q_ref.shape[0]
    hd = HEAD_DIM
    scale = hd ** -0.5
    kb[...] = k_ref[...].astype(BF16)
    vb[...] = v_ref[...].astype(BF16)
    lam = _lam_value(lam_ref, lam_init)
    row = lax.broadcasted_iota(jnp.int32, (tq, tq), 0) // CHUNK
    col = lax.broadcasted_iota(jnp.int32, (tq, tq), 1) // CHUNK
    diag_mask = col <= row
    for qb in range(seq // tq):
        st, end = qb * tq, (qb + 1) * tq
        probs, coef = [], []
        for m in range(2):
            qm = q_ref[st:end, m * hd:(m + 1) * hd]
            km = kb[0:end, m * hd:(m + 1) * hd]
            s = lax.dot_general(qm, km, NT_DIMS, preferred_element_type=F32)
            parts = [s[:, :st]] if st > 0 else []
            parts.append(jnp.where(diag_mask, s[:, st:], -1e30))
            ps, l = _softmax_parts(parts, scale)
            probs.append(ps)
            coef.append((1.0 / l) if m == 0 else (lam / l))
        o = None
        bounds = ([(0, st)] if st > 0 else []) + [(st, end)]
        for idx, (lo, hi) in enumerate(bounds):
            a = (probs[0][idx] * coef[0] - probs[1][idx] * coef[1]).astype(BF16)
            part = jnp.dot(a, vb[lo:hi, :], preferred_element_type=F32)
            o = part if o is None else o + part
        o_ref[st:end, :] = _subln(o, sub_ref, lam_init).astype(o_ref.dtype)


def _attn_prompt(q, k, v, lam_vecs, subln, batch, seq, lam_init, tq=256):
    t, width = q.shape
    hw = 2 * HEAD_DIM
    spec = pl.BlockSpec((seq, hw), lambda b, h: (b, h))
    return pl.pallas_call(
        functools.partial(_attn_prompt_kernel, tq=tq, lam_init=lam_init),
        grid=(batch, width // hw),
        in_specs=[spec, spec, spec,
                  pl.BlockSpec((4, HEAD_DIM), lambda b, h: (0, 0)),
                  pl.BlockSpec((1, hw), lambda b, h: (0, 0))],
        out_specs=spec,
        out_shape=jax.ShapeDtypeStruct((t, width), BF16),
        scratch_shapes=[pltpu.VMEM((seq, hw), BF16), pltpu.VMEM((seq, hw), BF16)],
        compiler_params=_cparams(("arbitrary", "arbitrary"), 56),
        name="attn_prompt",
    )(q, k, v, lam_vecs, subln.reshape(1, hw))


def _attn_sample_kernel(q_ref, k_ref, v_ref, pk_ref, pv_ref, lam_ref, sub_ref, o_ref, *, lam_init):
    hd = HEAD_DIM
    scale = hd ** -0.5
    lam = _lam_value(lam_ref, lam_init)
    pk = pk_ref[...].astype(BF16)
    kn = k_ref[...].astype(BF16)
    probs, coef = [], []
    for m in range(2):
        qm = q_ref[:, m * hd:(m + 1) * hd]
        parts = [lax.dot_general(qm, pk[:, m * hd:(m + 1) * hd], NT_DIMS, preferred_element_type=F32),
                 lax.dot_general(qm, kn[:, m * hd:(m + 1) * hd], NT_DIMS, preferred_element_type=F32)]
        ps, l = _softmax_parts(parts, scale)
        probs.append(ps)
        coef.append((1.0 / l) if m == 0 else (lam / l))
    vals = [pv_ref[...].astype(BF16), v_ref[...].astype(BF16)]
    o = None
    for idx in range(2):
        a = (probs[0][idx] * coef[0] - probs[1][idx] * coef[1]).astype(BF16)
        part = jnp.dot(a, vals[idx], preferred_element_type=F32)
        o = part if o is None else o + part
    o_ref[...] = _subln(o, sub_ref, lam_init).astype(o_ref.dtype)


def _attn_sample(q, k, v, past_k, past_v, lam_vecs, subln, batch, seq, lam_init):
    t, width = q.shape
    hw = 2 * HEAD_DIM
    past = past_k.shape[0] // batch
    spec = pl.BlockSpec((seq, hw), lambda b, h: (b, h))
    pspec = pl.BlockSpec((past, hw), lambda b, h: (b, h))
    return pl.pallas_call(
        functools.partial(_attn_sample_kernel, lam_init=lam_init),
        grid=(batch, width // hw),
        in_specs=[spec, spec, spec, pspec, pspec,
                  pl.BlockSpec((4, HEAD_DIM), lambda b, h: (0, 0)),
                  pl.BlockSpec((1, hw), lambda b, h: (0, 0))],
        out_specs=spec,
        out_shape=jax.ShapeDtypeStruct((t, width), BF16),
        compiler_params=_cparams(("arbitrary", "arbitrary"), 48),
        name="attn_sample",
    )(q, k, v, past_k, past_v, lam_vecs, subln.reshape(1, hw))


def _post_mixer(x, y, mod_ref, gf_ref, wr_ref, x1_ref, h2_ref, lg_ref):
    x1 = x + mod_ref[0, 2:3, :] * y
    h2 = _norm_mod(x1, gf_ref[...], mod_ref[0, 4:5, :], mod_ref[0, 3:4, :])
    x1_ref[...] = x1
    h2_ref[...] = h2
    h_hi = h2.astype(BF16)
    h_lo = (h2 - h_hi.astype(F32)).astype(BF16)
    w = wr_ref[...]
    w_hi = w.astype(BF16)
    w_lo = (w - w_hi.astype(F32)).astype(BF16)
    lg_ref[...] = (jnp.dot(h_hi, w_hi, preferred_element_type=F32)
                   + (jnp.dot(h_hi, w_lo, preferred_element_type=F32)
                      + jnp.dot(h_lo, w_hi, preferred_element_type=F32)))


def _wo_kernel(o_ref, w_ref, x_ref, mod_ref, gf_ref, wr_ref, x1_ref, h2_ref, lg_ref):
    y = jnp.dot(o_ref[...], w_ref[...], preferred_element_type=F32)
    _post_mixer(x_ref[...], y, mod_ref, gf_ref, wr_ref, x1_ref, h2_ref, lg_ref)


def _wo_post(o, w_bf, x2d, mod, gain_ffn, w_router, seq, tm):
    t, d = x2d.shape
    width = o.shape[1]
    e = w_router.shape[1]
    spt = seq // tm
    row = pl.BlockSpec((tm, d), lambda i: (i, 0))
    x1, h2, lg = pl.pallas_call(
        _wo_kernel,
        grid=(t // tm,),
        in_specs=[pl.BlockSpec((tm, width), lambda i: (i, 0)),
                  pl.BlockSpec((width, d), lambda i: (0, 0)),
                  row,
                  pl.BlockSpec((1, N_ADA, d), lambda i: (i // spt, 0, 0)),
                  pl.BlockSpec((1, d), lambda i: (0, 0)),
                  pl.BlockSpec((d, e), lambda i: (0, 0))],
        out_specs=[row, row, pl.BlockSpec((tm, e), lambda i: (i, 0))],
        out_shape=[jax.ShapeDtypeStruct((t, d), F32), jax.ShapeDtypeStruct((t, d), F32),
                   jax.ShapeDtypeStruct((t, e), F32)],
        compiler_params=_cparams(("arbitrary",), 48),
        name="wo_post",
    )(o, w_bf, x2d, mod, gain_ffn.reshape(1, d), w_router)
    return x1, h2, lg.T


def _pool_kernel(x_ref, xh_ref, prev_ref, mod_ref, gm_ref, wp_ref, ps_ref, gf_ref, wr_ref,
                 x1_ref, h2_ref, lg_ref, np_ref, hext, ybuf, *, pos0, tm):
    s = pl.program_id(1)
    halo = POOL_MAX
    x = x_ref[...]
    sc1, sh1 = mod_ref[0, 1:2, :], mod_ref[0, 0:1, :]
    h = _norm_mod(x, gm_ref[...], sc1, sh1)
    h_halo = _norm_mod(xh_ref[...], gm_ref[...], sc1, sh1)
    hext[0:halo, :] = jnp.where(s == 0, prev_ref[0], h_halo)
    hext[halo:, :] = h
    pos = pos0 + s * tm + lax.broadcasted_iota(jnp.int32, (tm, 1), 0)
    gc = x.shape[1] // len(POOL_WINDOWS)
    for g, w in enumerate(POOL_WINDOWS):
        c0, c1 = g * gc, (g + 1) * gc
        tot = hext[halo:halo + tm, c0:c1]
        for k in range(1, w):
            tot = tot + hext[halo - k:halo - k + tm, c0:c1]
        cnt = jnp.minimum(w, pos + 1).astype(F32)
        dlt = tot / cnt - hext[halo:halo + tm, c0:c1]
        yg = jnp.dot(dlt.astype(BF16), wp_ref[g], preferred_element_type=F32)
        ybuf[:, c0:c1] = yg * ps_ref[:, c0:c1]
    _post_mixer(x, ybuf[...], mod_ref, gf_ref, wr_ref, x1_ref, h2_ref, lg_ref)

    @pl.when(s == pl.num_programs(1) - 1)
    def _():
        np_ref[0] = hext[tm:tm + halo, :]


def _pool_post(x2d, prev, mod, gain_mix, wp_bf, pool_scale, gain_ffn, w_router, batch, seq, pos0, tm):
    t, d = x2d.shape
    e = w_router.shape[1]
    spt = seq // tm
    hpt = tm // POOL_MAX
    ng, gc = wp_bf.shape[0], wp_bf.shape[1]
    row = pl.BlockSpec((tm, d), lambda b, s: (b * spt + s, 0))
    vec = pl.BlockSpec((1, d), lambda b, s: (0, 0))
    x1, h2, lg, new_pool = pl.pallas_call(
        functools.partial(_pool_kernel, pos0=pos0, tm=tm),
        grid=(batch, spt),
        in_specs=[row,
                  pl.BlockSpec((POOL_MAX, d), lambda b, s: (jnp.maximum((b * spt + s) * hpt - 1, 0), 0)),
                  pl.BlockSpec((1, POOL_MAX, d), lambda b, s: (b, 0, 0)),
                  pl.BlockSpec((1, N_ADA, d), lambda b, s: (b, 0, 0)),
                  vec,
                  pl.BlockSpec((ng, gc, gc), lambda b, s: (0, 0, 0)),
                  vec, vec,
                  pl.BlockSpec((d, e), lambda b, s: (0, 0))],
        out_specs=[row, row, pl.BlockSpec((tm, e), lambda b, s: (b * spt + s, 0)),
                   pl.BlockSpec((1, POOL_MAX, d), lambda b, s: (b, 0, 0))],
        out_shape=[jax.ShapeDtypeStruct((t, d), F32), jax.ShapeDtypeStruct((t, d), F32),
                   jax.ShapeDtypeStruct((t, e), F32), jax.ShapeDtypeStruct((batch, POOL_MAX, d), F32)],
        scratch_shapes=[pltpu.VMEM((POOL_MAX + tm, d), F32), pltpu.VMEM((tm, d), F32)],
        compiler_params=_cparams(("arbitrary", "arbitrary"), 48),
        name="pool_post",
    )(x2d, x2d, prev, mod, gain_mix.reshape(1, d), wp_bf, pool_scale.reshape(1, d), gain_ffn.reshape(1, d), w_router)
    return x1, h2, lg.T, new_pool


def _route_kernel(lg_ref, bias_ref, idx_ref, gate_ref, rank_ref, cnt_ref, carry):
    i = pl.program_id(0)

    @pl.when(i == 0)
    def _():
        carry[...] = jnp.zeros_like(carry)

    score = jax.nn.sigmoid(lg_ref[...])
    n_e, tm = score.shape
    val = score + bias_ref[...]
    e_iota = lax.broadcasted_iota(jnp.int32, (n_e, tm), 0)
    sel = jnp.zeros((n_e, tm), F32)
    idxs, picked = [], []
    for _ in range(TOP_K):
        best = jnp.max(val, axis=0, keepdims=True)
        ix = jnp.min(jnp.where(val == best, e_iota, n_e), axis=0, keepdims=True)
        hit = e_iota == ix
        picked.append(jnp.sum(jnp.where(hit, score, 0.0), axis=0, keepdims=True))
        idxs.append(ix)
        val = jnp.where(hit, -jnp.inf, val)
        sel = jnp.where(hit, 1.0, sel)
    tri = (lax.broadcasted_iota(jnp.int32, (tm, tm), 0) <= lax.broadcasted_iota(jnp.int32, (tm, tm), 1)).astype(BF16)
    incl = jnp.dot(sel.astype(BF16), tri, preferred_element_type=F32)
    excl = incl - sel + carry[:, 0:1]
    total = picked[0]
    for p in picked[1:]:
        total = total + p
    for r in range(TOP_K):
        hit = e_iota == idxs[r]
        idx_ref[r:r + 1, :] = idxs[r]
        gate_ref[r:r + 1, :] = picked[r] / total * ROUTED_SCALE
        rank_ref[r:r + 1, :] = jnp.sum(jnp.where(hit, excl, 0.0), axis=0, keepdims=True).astype(jnp.int32)
    carry[...] = carry[...] + jnp.sum(sel, axis=1, keepdims=True)
    cnt_ref[...] = carry[...]


def _route(logits_t, bias, tm):
    e, t = logits_t.shape
    out = pl.BlockSpec((TOP_K, tm), lambda i: (0, i))
    return pl.pallas_call(
        _route_kernel,
        grid=(t // tm,),
        in_specs=[pl.BlockSpec((e, tm), lambda i: (0, i)), pl.BlockSpec((e, 1), lambda i: (0, 0))],
        out_specs=[out, out, out, pl.BlockSpec((e, LANES), lambda i: (0, 0))],
        out_shape=[jax.ShapeDtypeStruct((TOP_K, t), jnp.int32), jax.ShapeDtypeStruct((TOP_K, t), F32),
                   jax.ShapeDtypeStruct((TOP_K, t), jnp.int32), jax.ShapeDtypeStruct((e, LANES), F32)],
        scratch_shapes=[pltpu.VMEM((e, LANES), F32)],
        compiler_params=_cparams(("arbitrary",), 32),
        name="route_topk",
    )(logits_t, bias.reshape(e, 1))


DISPATCH_CHUNK = 32
N_DMA_PRIORITIES = 2


def _row_copy(src, src_row, dst, dst_row, sem):
    return pltpu.make_async_copy(src.at[pl.ds(src_row, 1), :], dst.at[pl.ds(dst_row, 1), :], sem)


def _dest_kernel(ps_ref, idx_ref, rank_ref, dest_ref):
    idx = idx_ref[...]
    dest = rank_ref[...]
    for e in range(ps_ref.shape[0]):
        dest = dest + jnp.where(idx == e, ps_ref[e], 0)
    dest_ref[...] = dest


def _dest_rows(idx_t, rank_t, pad_start, tm):
    k, t = idx_t.shape
    blk = pl.BlockSpec((k, tm), lambda i, ps: (0, i))
    return pl.pallas_call(
        _dest_kernel,
        grid_spec=pltpu.PrefetchScalarGridSpec(num_scalar_prefetch=1, grid=(t // tm,), in_specs=[blk, blk],
                                               out_specs=blk),
        out_shape=jax.ShapeDtypeStruct((k, t), jnp.int32),
        compiler_params=_cparams(("arbitrary",), 32),
        name="moe_dest",
    )(pad_start, idx_t, rank_t)


def _pack_bf16_pairs(x):
    half = x.shape[1] // 2
    lo = lax.bitcast_convert_type(x[:, :half].astype(BF16).astype(F32), jnp.uint32)
    hi = lax.bitcast_convert_type(x[:, half:].astype(BF16).astype(F32), jnp.uint32)
    return (lo >> 16) | (hi & jnp.uint32(0xFFFF0000))


def _unpack_bf16_pairs(w):
    lo = lax.bitcast_convert_type(w << 16, F32).astype(BF16)
    hi = lax.bitcast_convert_type(w & jnp.uint32(0xFFFF0000), F32).astype(BF16)
    return lo, hi


def _dispatch_kernel(dest_ref, h_ref, xs_ref, packed, sems, *, td):
    i = pl.program_id(0)
    slot = i % 2
    n_chunks = td // DISPATCH_CHUNK
    src = packed.at[slot]
    src[...] = _pack_bf16_pairs(h_ref[...])

    for c in range(n_chunks):
        for t in range(c * DISPATCH_CHUNK, (c + 1) * DISPATCH_CHUNK):
            for r in range(TOP_K):
                _row_copy(src, t, xs_ref, dest_ref[r, t], sems.at[slot, c]).start(priority=r % N_DMA_PRIORITIES)

    def wait_step(s):
        for c in range(n_chunks):
            for _ in range(DISPATCH_CHUNK * TOP_K):
                _row_copy(packed.at[s], 0, xs_ref, 0, sems.at[s, c]).wait()

    @pl.when(i > 0)
    def _():
        wait_step(1 - slot)

    @pl.when(i == pl.num_programs(0) - 1)
    def _():
        wait_step(slot)


def _dispatch_into_kernel(dest_ref, h_ref, xs_in_ref, xs_ref, packed, sems, *, td):
    del xs_in_ref
    _dispatch_kernel(dest_ref, h_ref, xs_ref, packed, sems, td=td)


def _dispatch(h2, dest_t, n_buf, td, xs_prev=None):
    t, d = h2.shape
    in_specs = [pl.BlockSpec((TOP_K, td), lambda i: (0, i), memory_space=pltpu.SMEM),
                pl.BlockSpec((td, d), lambda i: (i, 0))]
    args = [dest_t, h2]
    kern = functools.partial(_dispatch_kernel, td=td)
    aliases = {}
    if xs_prev is not None:
        in_specs.append(pl.BlockSpec(memory_space=pl.ANY))
        args.append(xs_prev)
        aliases = {2: 0}
        kern = functools.partial(_dispatch_into_kernel, td=td)
    return pl.pallas_call(
        kern,
        grid=(t // td,),
        in_specs=in_specs,
        out_specs=pl.BlockSpec(memory_space=pl.ANY),
        out_shape=jax.ShapeDtypeStruct((n_buf, d // 2), jnp.uint32),
        scratch_shapes=[pltpu.VMEM((2, td, d // 2), jnp.uint32),
                        pltpu.SemaphoreType.DMA((2, td // DISPATCH_CHUNK))],
        input_output_aliases=aliases,
        compiler_params=_cparams(("arbitrary",), 32),
        name="moe_dispatch",
    )(*args)


def _expert_kernel(be_ref, bi_ref, nv_ref, xs_ref, wgu_ref, wdn_ref, ys_ref, wgu_bf, wdn_bf, *, ff):
    i = pl.program_id(0)
    nv = nv_ref[i]

    @pl.when((i == 0) | (be_ref[i] != be_ref[jnp.maximum(i - 1, 0)]))
    def _():
        wgu_bf[...] = wgu_ref[0, 0].astype(BF16)
        wdn_bf[...] = wdn_ref[0, 0].astype(BF16)

    @pl.when(nv > 0)
    def _():
        w = xs_ref[...]
        rows = lax.broadcasted_iota(jnp.int32, (w.shape[0], 1), 0)
        w = jnp.where(rows < nv, w, jnp.uint32(0))
        x_lo, x_hi = _unpack_bf16_pairs(w)
        half = w.shape[1]
        au = (jnp.dot(x_lo, wgu_bf[0:half, :], preferred_element_type=F32)
              + jnp.dot(x_hi, wgu_bf[half:, :], preferred_element_type=F32))
        act = (jax.nn.silu(au[:, :ff]) * au[:, ff:]).astype(BF16)
        ys_ref[...] = jnp.dot(act, wdn_bf[...], preferred_element_type=F32)


def _experts(xs, w_gu, w_down, layer, block_e, block_i, block_nv, bm):
    n_buf = xs.shape[0]
    d, ff = w_gu.shape[2], w_down.shape[2]
    row_in = pl.BlockSpec((bm, d // 2), lambda i, be, bi, nv: (bi[i], 0))
    row = pl.BlockSpec((bm, d), lambda i, be, bi, nv: (bi[i], 0))
    return pl.pallas_call(
        functools.partial(_expert_kernel, ff=ff),
        grid_spec=pltpu.PrefetchScalarGridSpec(
            num_scalar_prefetch=3,
            grid=(n_buf // bm,),
            in_specs=[row_in,
                      pl.BlockSpec((1, 1, d, 2 * ff), lambda i, be, bi, nv: (layer, be[i], 0, 0)),
                      pl.BlockSpec((1, 1, ff, d), lambda i, be, bi, nv: (layer, be[i], 0, 0))],
            out_specs=row,
            scratch_shapes=[pltpu.VMEM((d, 2 * ff), BF16), pltpu.VMEM((ff, d), BF16)]),
        out_shape=jax.ShapeDtypeStruct((n_buf, d), F32),
        compiler_params=_cparams(("arbitrary",), 56),
        name="moe_experts",
    )(block_e, block_i, block_nv, xs, w_gu, w_down)


def _combine_kernel(dest_ref, dnext_ref, ys_ref, gate_ref, h2_ref, x1_ref, mod_ref, wsg_ref, wsd_ref, gfin_ref,
                    out_ref, buf, sems, *, tc, ff, final_norm, whole_index):
    i = pl.program_id(0)
    n = pl.num_programs(0)
    slot = i % 2

    def start_tile(table, tok0, s):
        for t in range(tc):
            for r in range(TOP_K):
                pltpu.make_async_copy(ys_ref.at[pl.ds(table[r, tok0 + t], 1), :], buf.at[s, r, pl.ds(t, 1), :],
                                      sems.at[s, r]).start(priority=r % N_DMA_PRIORITIES)

    @pl.when(i == 0)
    def _():
        start_tile(dest_ref, 0, 0)

    @pl.when(i + 1 < n)
    def _():
        start_tile(dnext_ref, (i + 1) * tc if whole_index else 0, 1 - slot)

    acc = _swiglu(h2_ref[...].astype(BF16), wsg_ref[...], wsd_ref[...], ff)

    for r in range(TOP_K):
        for _ in range(tc):
            pltpu.make_async_copy(ys_ref.at[pl.ds(0, 1), :], buf.at[slot, r, pl.ds(0, 1), :], sems.at[slot, r]).wait()

    gate = gate_ref[...]
    routed = gate[:, 0:1] * buf[slot, 0]
    for r in range(1, TOP_K):
        routed = routed + gate[:, r:r + 1] * buf[slot, r]
    x2 = x1_ref[...] + mod_ref[0, 5:6, :] * (routed + acc)
    if final_norm:
        x2 = _rms(x2, gfin_ref[...])
    out_ref[...] = x2


def _combine(ys, dest_t, gates, h2, x1, mod, ws_gu_bf, ws_dn_bf, gain_final, seq, tc, final_norm):
    t, d = x1.shape
    ff = ws_dn_bf.shape[0]
    spt = seq // tc
    n = t // tc
    whole_index = tc % LANES != 0
    if whole_index:
        smem = smem_next = pl.BlockSpec((TOP_K, t), lambda i: (0, 0), memory_space=pltpu.SMEM)
    else:
        smem = pl.BlockSpec((TOP_K, tc), lambda i: (0, i), memory_space=pltpu.SMEM)
        smem_next = pl.BlockSpec((TOP_K, tc), lambda i: (0, jnp.minimum(i + 1, n - 1)), memory_space=pltpu.SMEM)
    row = pl.BlockSpec((tc, d), lambda i: (i, 0))
    return pl.pallas_call(
        functools.partial(_combine_kernel, tc=tc, ff=ff, final_norm=final_norm, whole_index=whole_index),
        grid=(n,),
        in_specs=[smem, smem_next,
                  pl.BlockSpec(memory_space=pl.ANY),
                  pl.BlockSpec((tc, TOP_K), lambda i: (i, 0)),
                  row, row,
                  pl.BlockSpec((1, N_ADA, d), lambda i: (i // spt, 0, 0)),
                  pl.BlockSpec((d, 2 * ff), lambda i: (0, 0)),
                  pl.BlockSpec((ff, d), lambda i: (0, 0)),
                  pl.BlockSpec((1, d), lambda i: (0, 0))],
        out_specs=row,
        out_shape=jax.ShapeDtypeStruct((t, d), F32),
        scratch_shapes=[pltpu.VMEM((2, TOP_K, tc, d), F32), pltpu.SemaphoreType.DMA((2, TOP_K))],
        compiler_params=_cparams(("arbitrary",), 56),
        name="moe_combine",
    )(dest_t, dest_t, ys, gates, h2, x1, mod, ws_gu_bf, ws_dn_bf, gain_final.reshape(1, d))


def _block_tables(counts, bm, n_blocks):
    n_e = counts.shape[0]
    padded = (counts + bm - 1) // bm * bm
    pad_end = jnp.cumsum(padded)
    pad_start = pad_end - padded
    used = pad_end[-1] // bm
    blk = jnp.arange(n_blocks, dtype=jnp.int32)
    src = jnp.minimum(blk, jnp.maximum(used - 1, 0))
    owner = jnp.sum((pad_end[None, :] <= (src * bm)[:, None]).astype(jnp.int32), axis=1)
    block_e = jnp.minimum(owner, n_e - 1)
    valid = jnp.clip(counts[block_e] - (src * bm - pad_start[block_e]), 0, bm)
    block_nv = jnp.where(blk < used, valid, 0).astype(jnp.int32)
    return pad_start.astype(jnp.int32), block_e.astype(jnp.int32), src.astype(jnp.int32), block_nv


def _moe(groups, bias, w_exp_gu, w_exp_down, layer, ws_gu_bf, ws_dn_bf, gain_final, tiles, final_norm):
    n_e = groups[0]["logits_t"].shape[0]
    sizes = [g["x1"].shape[0] for g in groups]
    t_all = sum(sizes)
    bm = tiles["bm"]
    logits_t = jnp.concatenate([g["logits_t"] for g in groups], axis=1)
    idx_t, gate_t, rank_t, cnt = _route(logits_t, bias, tiles["route"])
    counts = cnt[:, 0].astype(jnp.int32)
    n_blocks = -(-(t_all * TOP_K + n_e * (bm - 1)) // bm)
    pad_start, block_e, block_i, block_nv = _block_tables(counts, bm, n_blocks)
    dest_t = _dest_rows(idx_t, rank_t, pad_start, tiles["route"])
    offs = [sum(sizes[:k]) for k in range(len(groups))]
    xs = None
    for g, off, t in zip(groups, offs, sizes):
        xs = _dispatch(g["h2"], dest_t[:, off:off + t], n_blocks * bm, g["dispatch"], xs)
    ys = _experts(xs, w_exp_gu, w_exp_down, layer, block_e, block_i, block_nv, bm)
    return [_combine(ys, dest_t[:, off:off + t], gate_t[:, off:off + t].T, g["h2"], g["x1"], g["mod"], ws_gu_bf,
                     ws_dn_bf, gain_final, g["seq"], g["combine"], final_norm)
            for g, off, t in zip(groups, offs, sizes)]


PROMPT_TILES = dict(qkv=256, wo=256, pool=256, dispatch=256, combine=128)
SAMPLE_TILES = dict(qkv=32, wo=32, pool=32, dispatch=256, combine=32)
MOE_TILES = dict(route=256, bm=512)


def _mixer(st, i, p):
    b, s, d = st["shape"]
    x2d, mod, tiles = st["x2d"], st["mods"][i], st["tiles"]
    w_router = p["w_router"][i]
    if i % N_MIXERS == 0:
        a = i // N_MIXERS
        lam_init = 0.8 - 0.6 * math.exp(-0.3 * i)
        lam_vecs = jnp.stack([p["lambda_q1"][a], p["lambda_k1"][a], p["lambda_q2"][a], p["lambda_k2"][a]])
        q, k, v = _qkv(x2d, mod, p["norm_mix"][i], p["w_qkv_bf"][a], st["tabs"], s, tiles["qkv"])
        width = k.shape[1]
        if st["past_k"] is None:
            o = _attn_prompt(q, k, v, lam_vecs, p["subln_gain"][a], b, s, lam_init)
        else:
            o = _attn_sample(q, k, v, st["past_k"][a].reshape(-1, width), st["past_v"][a].reshape(-1, width),
                             lam_vecs, p["subln_gain"][a], b, s, lam_init)
        st["new_k"].append(k.reshape(b, s, width // HEAD_DIM, HEAD_DIM))
        st["new_v"].append(v.reshape(b, s, width // (2 * HEAD_DIM), 2 * HEAD_DIM))
        x1, h2, lg = _wo_post(o, p["w_o_bf"][a], x2d, mod, p["norm_ffn"][i], w_router, s, tiles["wo"])
    else:
        pi = i // N_MIXERS
        if st["past_pool"] is None:
            prev = jnp.zeros((b, POOL_MAX, d), F32)
        else:
            prev = jnp.concatenate([jnp.zeros((b, 1, d), F32), st["past_pool"][pi]], axis=1)
        x1, h2, lg, npool = _pool_post(x2d, prev, mod, p["norm_mix"][i], p["w_pool_bf"][pi], p["pool_scale"][pi],
                                       p["norm_ffn"][i], w_router, b, s, st["pos0"], tiles["pool"])
        st["new_pool"].append(npool[:, 1:, :])
    return dict(x1=x1, h2=h2, logits_t=lg, mod=mod, seq=s, dispatch=tiles["dispatch"], combine=tiles["combine"])


def _stream(x, mods, pos0, past_k, past_v, past_pool, tiles):
    b, s, d = x.shape
    return dict(shape=(b, s, d), x2d=x.reshape(b * s, d), mods=mods, pos0=pos0, tiles=tiles,
                tabs=_rope_tables(pos0 + jnp.arange(s, dtype=jnp.int32)),
                past_k=past_k, past_v=past_v, past_pool=past_pool, new_k=[], new_v=[], new_pool=[])


def kernel(x_prompt, x_sample, c_prompt, c_sample, cache_k, cache_v, state_pool, w_ada, b_ada, norm_mix, norm_ffn,
           norm_final, w_qkv, w_o, lambda_q1, lambda_k1, lambda_q2, lambda_k2, subln_gain, w_pool, pool_scale,
           w_router, router_bias, w_exp_gu, w_exp_down, w_sh_gu, w_sh_down):
    depth, d = norm_mix.shape
    nb = c_prompt.shape[0]
    mod_all = _ada(jnp.concatenate([c_prompt, c_sample], axis=0), w_ada, b_ada)
    mod_all = mod_all.reshape(depth, -1, N_ADA, d)
    p = dict(norm_mix=norm_mix, norm_ffn=norm_ffn, norm_final=norm_final,
             lambda_q1=lambda_q1, lambda_k1=lambda_k1, lambda_q2=lambda_q2, lambda_k2=lambda_k2,
             subln_gain=subln_gain, pool_scale=pool_scale, w_router=w_router, router_bias=router_bias,
             w_qkv_bf=w_qkv.astype(BF16), w_o_bf=w_o.astype(BF16), w_pool_bf=w_pool.astype(BF16),
             w_sh_gu_bf=w_sh_gu.astype(BF16), w_sh_down_bf=w_sh_down.astype(BF16))
    streams = [_stream(x_prompt, [mod_all[i, :nb] for i in range(depth)], 0, None, None, None, PROMPT_TILES),
               _stream(x_sample, [mod_all[i, nb:] for i in range(depth)], cache_k.shape[2], cache_k, cache_v,
                       state_pool, SAMPLE_TILES)]
    for i in range(depth):
        groups = [_mixer(st, i, p) for st in streams]
        outs = _moe(groups, router_bias[i], w_exp_gu, w_exp_down, i, p["w_sh_gu_bf"][i], p["w_sh_down_bf"][i],
                    norm_final, MOE_TILES, final_norm=(i == depth - 1))
        for st, x2d in zip(streams, outs):
            st["x2d"] = x2d
    res = [(st["x2d"].reshape(st["shape"]), jnp.stack(st["new_k"]), jnp.stack(st["new_v"]), jnp.stack(st["new_pool"]))
           for st in streams]
    (y_p, k_p, v_p, pool_p), (y_s, k_s, v_s, pool_s) = res
    return (y_p, y_s, k_p, v_p, pool_p, k_s, v_s, pool_s)
```

```python
import functools
import math

import jax
import jax.numpy as jnp
from jax import lax
from jax.experimental import pallas as pl
from jax.experimental.pallas import tpu as pltpu

CHUNK = 64
HEAD_DIM = 128
ROT_DIM = HEAD_DIM // 4
ROPE_THETA = 500000.0
POOL_WINDOWS = (2, 4, 8, 16)
POOL_MAX = 16
TOP_K = 8
ROUTED_SCALE = 2.5
NORM_EPS = 1e-6
N_ADA = 6
N_MIXERS = 2

LANES = 128
LOG2_E = math.log2(math.e)
MIB = 1024 * 1024

F32 = jnp.float32
BF16 = jnp.bfloat16
NT_DIMS = (((1,), (1,)), ((), ()))


def _cparams(sem, vmem_mib):
    return pltpu.CompilerParams(dimension_semantics=sem, vmem_limit_bytes=vmem_mib * MIB)


def _rms(x, gain):
    return x * lax.rsqrt(jnp.mean(x * x, axis=-1, keepdims=True) + NORM_EPS) * gain


def _norm_mod(x, gain, scale, shift):
    return _rms(x, gain) * (1.0 + scale) + shift


def _swiglu(x_bf, w_gu, w_down, ff):
    au = jnp.dot(x_bf, w_gu, preferred_element_type=F32)
    act = (jax.nn.silu(au[:, :ff]) * au[:, ff:]).astype(BF16)
    return jnp.dot(act, w_down, preferred_element_type=F32)


def _ada_kernel(c_ref, w_ref, b_ref, o_ref):
    c_act = jax.nn.silu(c_ref[...]).astype(BF16)
    o_ref[0] = jnp.dot(c_act, w_ref[0].astype(BF16), preferred_element_type=F32) + b_ref[0]


def _ada(c_all, w_ada, b_ada, tn=1024):
    depth, d, n = w_ada.shape
    r = c_all.shape[0]
    return pl.pallas_call(
        _ada_kernel,
        grid=(depth, n // tn),
        in_specs=[pl.BlockSpec((r, d), lambda i, j: (0, 0)),
                  pl.BlockSpec((1, d, tn), lambda i, j: (i, 0, j)),
                  pl.BlockSpec((1, 1, tn), lambda i, j: (i, 0, j))],
        out_specs=pl.BlockSpec((1, r, tn), lambda i, j: (i, 0, j)),
        out_shape=jax.ShapeDtypeStruct((depth, r, n), F32),
        compiler_params=_cparams(("arbitrary", "arbitrary"), 40),
        name="ada_mod",
    )(c_all, w_ada, b_ada.reshape(depth, 1, n))


def _qkv_kernel(x_ref, mod_ref, g_ref, w_hbm, cos_ref, sa_ref, sb_ref, q_ref, k_ref, v_ref, w_vmem, h_scr, sem, *, tn):
    @pl.when(pl.program_id(0) == 0)
    def _():
        load = pltpu.make_async_copy(w_hbm, w_vmem, sem)
        load.start()
        load.wait()

    h_scr[...] = _norm_mod(x_ref[...], g_ref[...], mod_ref[0, 1:2, :], mod_ref[0, 0:1, :]).astype(BF16)
    width = q_ref.shape[1]
    cos, sa, sb = cos_ref[...], sa_ref[...], sb_ref[...]

    def rope_store(acc, dst, off):
        for c in range(tn // HEAD_DIM):
            blk = acc[:, c * HEAD_DIM:(c + 1) * HEAD_DIM]
            rot = blk * cos + pltpu.roll(blk, HEAD_DIM - ROT_DIM // 2, 1) * sa + pltpu.roll(blk, ROT_DIM // 2, 1) * sb
            dst[:, off + c * HEAD_DIM:off + (c + 1) * HEAD_DIM] = rot.astype(dst.dtype)

    for j in range(3 * width // tn):
        acc = jnp.dot(h_scr[...], w_vmem[:, j * tn:(j + 1) * tn], preferred_element_type=F32)
        part, off = divmod(j * tn, width)
        if part == 0:
            rope_store(acc, q_ref, off)
        elif part == 1:
            rope_store(acc, k_ref, off)
        else:
            v_ref[:, off:off + tn] = acc


def _qkv(x2d, mod, gain, w_bf, tabs, seq, tm, tn=512):
    t, d = x2d.shape
    w3 = w_bf.shape[1]
    width = w3 // 3
    spt = seq // tm
    cos_t, sa_t, sb_t = tabs
    tab_spec = pl.BlockSpec((tm, HEAD_DIM), lambda i: (i % spt, 0))
    out_spec = pl.BlockSpec((tm, width), lambda i: (i, 0))
    return pl.pallas_call(
        functools.partial(_qkv_kernel, tn=tn),
        grid=(t // tm,),
        in_specs=[pl.BlockSpec((tm, d), lambda i: (i, 0)),
                  pl.BlockSpec((1, N_ADA, d), lambda i: (i // spt, 0, 0)),
                  pl.BlockSpec((1, d), lambda i: (0, 0)),
                  pl.BlockSpec(memory_space=pl.ANY),
                  tab_spec, tab_spec, tab_spec],
        out_specs=[out_spec, out_spec, out_spec],
        out_shape=[jax.ShapeDtypeStruct((t, width), BF16),
                   jax.ShapeDtypeStruct((t, width), F32),
                   jax.ShapeDtypeStruct((t, width), F32)],
        scratch_shapes=[pltpu.VMEM((d, w3), BF16), pltpu.VMEM((tm, d), BF16), pltpu.SemaphoreType.DMA(())],
        compiler_params=_cparams(("arbitrary",), 56),
        name="qkv_rope",
    )(x2d, mod, gain.reshape(1, d), w_bf, cos_t, sa_t, sb_t)


def _rope_tables(pos):
    half = ROT_DIM // 2
    inv_freq = ROPE_THETA ** (-jnp.arange(half, dtype=F32) / half)
    ang = pos.astype(F32)[:, None] * inv_freq[None, :]
    cos, sin = jnp.cos(ang), jnp.sin(ang)
    s = pos.shape[0]
    rest = HEAD_DIM - ROT_DIM
    cos_t = jnp.concatenate([cos, cos, jnp.ones((s, rest), F32)], axis=1)
    sa_t = jnp.concatenate([-sin, jnp.zeros((s, half + rest), F32)], axis=1)
    sb_t = jnp.concatenate([jnp.zeros((s, half), F32), sin, jnp.zeros((s, rest), F32)], axis=1)
    return cos_t, sa_t, sb_t


def _lam_value(lam_ref, lam_init):
    l1 = jnp.sum(lam_ref[0:1, :] * lam_ref[1:2, :], axis=1, keepdims=True)
    l2 = jnp.sum(lam_ref[2:3, :] * lam_ref[3:4, :], axis=1, keepdims=True)
    return jnp.exp(l1) - jnp.exp(l2) + lam_init


def _softmax_parts(parts, scale):
    mx = None
    for s in parts:
        m = jnp.max(s, axis=1, keepdims=True)
        mx = m if mx is None else jnp.maximum(mx, m)
    ps = [jnp.exp2((s - mx) * (scale * LOG2_E)) for s in parts]
    l = None
    for p in ps:
        r = jnp.sum(p, axis=1, keepdims=True)
        l = r if l is None else l + r
    return ps, l


def _subln(o, sub_ref, lam_init):
    return _rms(o, sub_ref[...]) * (1.0 - lam_init)


def _attn_prompt_kernel(q_ref, k_ref, v_ref, lam_ref, sub_ref, o_ref, kb, vb, *, tq, lam_init):
    seq = q_ref.shape[0]
    hd = HEAD_DIM
    scale = hd ** -0.5
    kb[...] = k_ref[...].astype(BF16)
    vb[...] = v_ref[...].astype(BF16)
    lam = _lam_value(lam_ref, lam_init)
    row = lax.broadcasted_iota(jnp.int32, (tq, tq), 0) // CHUNK
    col = lax.broadcasted_iota(jnp.int32, (tq, tq), 1) // CHUNK
    diag_mask = col <= row
    for qb in range(seq // tq):
        st, end = qb * tq, (qb + 1) * tq
        probs, coef = [], []
        for m in range(2):
            qm = q_ref[st:end, m * hd:(m + 1) * hd]
            km = kb[0:end, m * hd:(m + 1) * hd]
            s = lax.dot_general(qm, km, NT_DIMS, preferred_element_type=F32)
            parts = [s[:, :st]] if st > 0 else []
            parts.append(jnp.where(diag_mask, s[:, st:], -1e30))
            ps, l = _softmax_parts(parts, scale)
            probs.append(ps)
            coef.append((1.0 / l) if m == 0 else (lam / l))
        o = None
        bounds = ([(0, st)] if st > 0 else []) + [(st, end)]
        for idx, (lo, hi) in enumerate(bounds):
            a = (probs[0][idx] * coef[0] - probs[1][idx] * coef[1]).astype(BF16)
            part = jnp.dot(a, vb[lo:hi, :], preferred_element_type=F32)
            o = part if o is None else o + part
        o_ref[st:end, :] = _subln(o, sub_ref, lam_init).astype(o_ref.dtype)


def _attn_prompt(q, k, v, lam_vecs, subln, batch, seq, lam_init, tq=256):
    t, width = q.shape
    hw = 2 * HEAD_DIM
    spec = pl.BlockSpec((seq, hw), lambda b, h: (b, h))
    return pl.pallas_call(
        functools.partial(_attn_prompt_kernel, tq=tq, lam_init=lam_init),
        grid=(batch, width // hw),
        in_specs=[spec, spec, spec,
                  pl.BlockSpec((4, HEAD_DIM), lambda b, h: (0, 0)),
                  pl.BlockSpec((1, hw), lambda b, h: (0, 0))],
        out_specs=spec,
        out_shape=jax.ShapeDtypeStruct((t, width), BF16),
        scratch_shapes=[pltpu.VMEM((seq, hw), BF16), pltpu.VMEM((seq, hw), BF16)],
        compiler_params=_cparams(("arbitrary", "arbitrary"), 56),
        name="attn_prompt",
    )(q, k, v, lam_vecs, subln.reshape(1, hw))


def _attn_sample_kernel(q_ref, k_ref, v_ref, pk_ref, pv_ref, lam_ref, sub_ref, o_ref, *, lam_init):
    hd = HEAD_DIM
    scale = hd ** -0.5
    lam = _lam_value(lam_ref, lam_init)
    pk = pk_ref[...].astype(BF16)
    kn = k_ref[...].astype(BF16)
    probs, coef = [], []
    for m in range(2):
        qm = q_ref[:, m * hd:(m + 1) * hd]
        parts = [lax.dot_general(qm, pk[:, m * hd:(m + 1) * hd], NT_DIMS, preferred_element_type=F32),
                 lax.dot_general(qm, kn[:, m * hd:(m + 1) * hd], NT_DIMS, preferred_element_type=F32)]
        ps, l = _softmax_parts(parts, scale)
        probs.append(ps)
        coef.append((1.0 / l) if m == 0 else (lam / l))
    vals = [pv_ref[...].astype(BF16), v_ref[...].astype(BF16)]
    o = None
    for idx in range(2):
        a = (probs[0][idx] * coef[0] - probs[1][idx] * coef[1]).astype(BF16)
        part = jnp.dot(a, vals[idx], preferred_element_type=F32)
        o = part if o is None else o + part
    o_ref[...] = _subln(o, sub_ref, lam_init).astype(o_ref.dtype)


def _attn_sample(q, k, v, past_k, past_v, lam_vecs, subln, batch, seq, lam_init):
    t, width = q.shape
    hw = 2 * HEAD_DIM
    past = past_k.shape[0] // batch
    spec = pl.BlockSpec((seq, hw), lambda b, h: (b, h))
    pspec = pl.BlockSpec((past, hw), lambda b, h: (b, h))
    return pl.pallas_call(
        functools.partial(_attn_sample_kernel, lam_init=lam_init),
        grid=(batch, width // hw),
        in_specs=[spec, spec, spec, pspec, pspec,
                  pl.BlockSpec((4, HEAD_DIM), lambda b, h: (0, 0)),
                  pl.BlockSpec((1, hw), lambda b, h: (0, 0))],
        out_specs=spec,
        out_shape=jax.ShapeDtypeStruct((t, width), BF16),
        compiler_params=_cparams(("arbitrary", "arbitrary"), 48),
        name="attn_sample",
    )(q, k, v, past_k, past_v, lam_vecs, subln.reshape(1, hw))


def _post_mixer(x, y, mod_ref, gf_ref, wr_ref, x1_ref, h2_ref, lg_ref):
    x1 = x + mod_ref[0, 2:3, :] * y
    h2 = _norm_mod(x1, gf_ref[...], mod_ref[0, 4:5, :], mod_ref[0, 3:4, :])
    x1_ref[...] = x1
    h2_ref[...] = h2
    h_hi = h2.astype(BF16)
    h_lo = (h2 - h_hi.astype(F32)).astype(BF16)
    w = wr_ref[...]
    w_hi = w.astype(BF16)
    w_lo = (w - w_hi.astype(F32)).astype(BF16)
    lg_ref[...] = (jnp.dot(h_hi, w_hi, preferred_element_type=F32)
                   + (jnp.dot(h_hi, w_lo, preferred_element_type=F32)
                      + jnp.dot(h_lo, w_hi, preferred_element_type=F32)))


def _wo_kernel(o_ref, w_ref, x_ref, mod_ref, gf_ref, wr_ref, x1_ref, h2_ref, lg_ref):
    y = jnp.dot(o_ref[...], w_ref[...], preferred_element_type=F32)
    _post_mixer(x_ref[...], y, mod_ref, gf_ref, wr_ref, x1_ref, h2_ref, lg_ref)


def _wo_post(o, w_bf, x2d, mod, gain_ffn, w_router, seq, tm):
    t, d = x2d.shape
    width = o.shape[1]
    e = w_router.shape[1]
    spt = seq // tm
    row = pl.BlockSpec((tm, d), lambda i: (i, 0))
    x1, h2, lg = pl.pallas_call(
        _wo_kernel,
        grid=(t // tm,),
        in_specs=[pl.BlockSpec((tm, width), lambda i: (i, 0)),
                  pl.BlockSpec((width, d), lambda i: (0, 0)),
                  row,
                  pl.BlockSpec((1, N_ADA, d), lambda i: (i // spt, 0, 0)),
                  pl.BlockSpec((1, d), lambda i: (0, 0)),
                  pl.BlockSpec((d, e), lambda i: (0, 0))],
        out_specs=[row, row, pl.BlockSpec((tm, e), lambda i: (i, 0))],
        out_shape=[jax.ShapeDtypeStruct((t, d), F32), jax.ShapeDtypeStruct((t, d), F32),
                   jax.ShapeDtypeStruct((t, e), F32)],
        compiler_params=_cparams(("arbitrary",), 48),
        name="wo_post",
    )(o, w_bf, x2d, mod, gain_ffn.reshape(1, d), w_router)
    return x1, h2, lg.T


def _pool_kernel(x_ref, xh_ref, prev_ref, mod_ref, gm_ref, wp_ref, ps_ref, gf_ref, wr_ref,
                 x1_ref, h2_ref, lg_ref, np_ref, hext, ybuf, *, pos0, tm):
    s = pl.program_id(1)
    halo = POOL_MAX
    x = x_ref[...]
    sc1, sh1 = mod_ref[0, 1:2, :], mod_ref[0, 0:1, :]
    h = _norm_mod(x, gm_ref[...], sc1, sh1)
    h_halo = _norm_mod(xh_ref[...], gm_ref[...], sc1, sh1)
    hext[0:halo, :] = jnp.where(s == 0, prev_ref[0], h_halo)
    hext[halo:, :] = h
    pos = pos0 + s * tm + lax.broadcasted_iota(jnp.int32, (tm, 1), 0)
    gc = x.shape[1] // len(POOL_WINDOWS)
    for g, w in enumerate(POOL_WINDOWS):
        c0, c1 = g * gc, (g + 1) * gc
        tot = hext[halo:halo + tm, c0:c1]
        for k in range(1, w):
            tot = tot + hext[halo - k:halo - k + tm, c0:c1]
        cnt = jnp.minimum(w, pos + 1).astype(F32)
        dlt = tot / cnt - hext[halo:halo + tm, c0:c1]
        yg = jnp.dot(dlt.astype(BF16), wp_ref[g], preferred_element_type=F32)
        ybuf[:, c0:c1] = yg * ps_ref[:, c0:c1]
    _post_mixer(x, ybuf[...], mod_ref, gf_ref, wr_ref, x1_ref, h2_ref, lg_ref)

    @pl.when(s == pl.num_programs(1) - 1)
    def _():
        np_ref[0] = hext[tm:tm + halo, :]


def _pool_post(x2d, prev, mod, gain_mix, wp_bf, pool_scale, gain_ffn, w_router, batch, seq, pos0, tm):
    t, d = x2d.shape
    e = w_router.shape[1]
    spt = seq // tm
    hpt = tm // POOL_MAX
    ng, gc = wp_bf.shape[0], wp_bf.shape[1]
    row = pl.BlockSpec((tm, d), lambda b, s: (b * spt + s, 0))
    vec = pl.BlockSpec((1, d), lambda b, s: (0, 0))
    x1, h2, lg, new_pool = pl.pallas_call(
        functools.partial(_pool_kernel, pos0=pos0, tm=tm),
        grid=(batch, spt),
        in_specs=[row,
                  pl.BlockSpec((POOL_MAX, d), lambda b, s: (jnp.maximum((b * spt + s) * hpt - 1, 0), 0)),
                  pl.BlockSpec((1, POOL_MAX, d), lambda b, s: (b, 0, 0)),
                  pl.BlockSpec((1, N_ADA, d), lambda b, s: (b, 0, 0)),
                  vec,
                  pl.BlockSpec((ng, gc, gc), lambda b, s: (0, 0, 0)),
                  vec, vec,
                  pl.BlockSpec((d, e), lambda b, s: (0, 0))],
        out_specs=[row, row, pl.BlockSpec((tm, e), lambda b, s: (b * spt + s, 0)),
                   pl.BlockSpec((1, POOL_MAX, d), lambda b, s: (b, 0, 0))],
        out_shape=[jax.ShapeDtypeStruct((t, d), F32), jax.ShapeDtypeStruct((t, d), F32),
                   jax.ShapeDtypeStruct((t, e), F32), jax.ShapeDtypeStruct((batch, POOL_MAX, d), F32)],
        scratch_shapes=[pltpu.VMEM((POOL_MAX + tm, d), F32), pltpu.VMEM((tm, d), F32)],
        compiler_params=_cparams(("arbitrary", "arbitrary"), 48),
        name="pool_post",
    )(x2d, x2d, prev, mod, gain_mix.reshape(1, d), wp_bf, pool_scale.reshape(1, d), gain_ffn.reshape(1, d), w_router)
    return x1, h2, lg.T, new_pool


def _route_kernel(lg_ref, bias_ref, idx_ref, gate_ref, rank_ref, cnt_ref, carry):
    i = pl.program_id(0)

    @pl.when(i == 0)
    def _():
        carry[...] = jnp.zeros_like(carry)

    score = jax.nn.sigmoid(lg_ref[...])
    n_e, tm = score.shape
    val = score + bias_ref[...]
    e_iota = lax.broadcasted_iota(jnp.int32, (n_e, tm), 0)
    sel = jnp.zeros((n_e, tm), F32)
    idxs, picked = [], []
    for _ in range(TOP_K):
        best = jnp.max(val, axis=0, keepdims=True)
        ix = jnp.min(jnp.where(val == best, e_iota, n_e), axis=0, keepdims=True)
        hit = e_iota == ix
        picked.append(jnp.sum(jnp.where(hit, score, 0.0), axis=0, keepdims=True))
        idxs.append(ix)
        val = jnp.where(hit, -jnp.inf, val)
        sel = jnp.where(hit, 1.0, sel)
    tri = (lax.broadcasted_iota(jnp.int32, (tm, tm), 0) <= lax.broadcasted_iota(jnp.int32, (tm, tm), 1)).astype(BF16)
    incl = jnp.dot(sel.astype(BF16), tri, preferred_element_type=F32)
    excl = incl - sel + carry[:, 0:1]
    total = picked[0]
    for p in picked[1:]:
        total = total + p
    for r in range(TOP_K):
        hit = e_iota == idxs[r]
        idx_ref[r:r + 1, :] = idxs[r]
        gate_ref[r:r + 1, :] = picked[r] / total * ROUTED_SCALE
        rank_ref[r:r + 1, :] = jnp.sum(jnp.where(hit, excl, 0.0), axis=0, keepdims=True).astype(jnp.int32)
    carry[...] = carry[...] + jnp.sum(sel, axis=1, keepdims=True)
    cnt_ref[...] = carry[...]


def _route(logits_t, bias, tm):
    e, t = logits_t.shape
    out = pl.BlockSpec((TOP_K, tm), lambda i: (0, i))
    return pl.pallas_call(
        _route_kernel,
        grid=(t // tm,),
        in_specs=[pl.BlockSpec((e, tm), lambda i: (0, i)), pl.BlockSpec((e, 1), lambda i: (0, 0))],
        out_specs=[out, out, out, pl.BlockSpec((e, LANES), lambda i: (0, 0))],
        out_shape=[jax.ShapeDtypeStruct((TOP_K, t), jnp.int32), jax.ShapeDtypeStruct((TOP_K, t), F32),
                   jax.ShapeDtypeStruct((TOP_K, t), jnp.int32), jax.ShapeDtypeStruct((e, LANES), F32)],
        scratch_shapes=[pltpu.VMEM((e, LANES), F32)],
        compiler_params=_cparams(("arbitrary",), 32),
        name="route_topk",
    )(logits_t, bias.reshape(e, 1))


DISPATCH_CHUNK = 32
N_DMA_PRIORITIES = 2


def _row_copy(src, src_row, dst, dst_row, sem):
    return pltpu.make_async_copy(src.at[pl.ds(src_row, 1), :], dst.at[pl.ds(dst_row, 1), :], sem)


def _dest_kernel(ps_ref, idx_ref, rank_ref, dest_ref):
    idx = idx_ref[...]
    dest = rank_ref[...]
    for e in range(ps_ref.shape[0]):
        dest = dest + jnp.where(idx == e, ps_ref[e], 0)
    dest_ref[...] = dest


def _dest_rows(idx_t, rank_t, pad_start, tm):
    k, t = idx_t.shape
    blk = pl.BlockSpec((k, tm), lambda i, ps: (0, i))
    return pl.pallas_call(
        _dest_kernel,
        grid_spec=pltpu.PrefetchScalarGridSpec(num_scalar_prefetch=1, grid=(t // tm,), in_specs=[blk, blk],
                                               out_specs=blk),
        out_shape=jax.ShapeDtypeStruct((k, t), jnp.int32),
        compiler_params=_cparams(("arbitrary",), 32),
        name="moe_dest",
    )(pad_start, idx_t, rank_t)


def _pack_bf16_pairs(x):
    half = x.shape[1] // 2
    lo = lax.bitcast_convert_type(x[:, :half].astype(BF16).astype(F32), jnp.uint32)
    hi = lax.bitcast_convert_type(x[:, half:].astype(BF16).astype(F32), jnp.uint32)
    return (lo >> 16) | (hi & jnp.uint32(0xFFFF0000))


def _unpack_bf16_pairs(w):
    lo = lax.bitcast_convert_type(w << 16, F32).astype(BF16)
    hi = lax.bitcast_convert_type(w & jnp.uint32(0xFFFF0000), F32).astype(BF16)
    return lo, hi


def _dispatch_kernel(dest_ref, h_ref, xs_ref, packed, sems, *, td):
    i = pl.program_id(0)
    slot = i % 2
    n_chunks = td // DISPATCH_CHUNK
    src = packed.at[slot]
    src[...] = _pack_bf16_pairs(h_ref[...])

    for c in range(n_chunks):
        for t in range(c * DISPATCH_CHUNK, (c + 1) * DISPATCH_CHUNK):
            for r in range(TOP_K):
                _row_copy(src, t, xs_ref, dest_ref[r, t], sems.at[slot, c]).start(priority=r % N_DMA_PRIORITIES)

    def wait_step(s):
        for c in range(n_chunks):
            for _ in range(DISPATCH_CHUNK * TOP_K):
                _row_copy(packed.at[s], 0, xs_ref, 0, sems.at[s, c]).wait()

    @pl.when(i > 0)
    def _():
        wait_step(1 - slot)

    @pl.when(i == pl.num_programs(0) - 1)
    def _():
        wait_step(slot)


def _dispatch_into_kernel(dest_ref, h_ref, xs_in_ref, xs_ref, packed, sems, *, td):
    del xs_in_ref
    _dispatch_kernel(dest_ref, h_ref, xs_ref, packed, sems, td=td)


def _dispatch(h2, dest_t, n_buf, td, xs_prev=None):
    t, d = h2.shape
    in_specs = [pl.BlockSpec((TOP_K, td), lambda i: (0, i), memory_space=pltpu.SMEM),
                pl.BlockSpec((td, d), lambda i: (i, 0))]
    args = [dest_t, h2]
    kern = functools.partial(_dispatch_kernel, td=td)
    aliases = {}
    if xs_prev is not None:
        in_specs.append(pl.BlockSpec(memory_space=pl.ANY))
        args.append(xs_prev)
        aliases = {2: 0}
        kern = functools.partial(_dispatch_into_kernel, td=td)
    return pl.pallas_call(
        kern,
        grid=(t // td,),
        in_specs=in_specs,
        out_specs=pl.BlockSpec(memory_space=pl.ANY),
        out_shape=jax.ShapeDtypeStruct((n_buf, d // 2), jnp.uint32),
        scratch_shapes=[pltpu.VMEM((2, td, d // 2), jnp.uint32),
                        pltpu.SemaphoreType.DMA((2, td // DISPATCH_CHUNK))],
        input_output_aliases=aliases,
        compiler_params=_cparams(("arbitrary",), 32),
        name="moe_dispatch",
    )(*args)


def _expert_kernel(be_ref, bi_ref, nv_ref, xs_ref, wgu_ref, wdn_ref, ys_ref, wgu_bf, wdn_bf, *, ff):
    i = pl.program_id(0)
    nv = nv_ref[i]

    @pl.when((i == 0) | (be_ref[i] != be_ref[jnp.maximum(i - 1, 0)]))
    def _():
        wgu_bf[...] = wgu_ref[0, 0].astype(BF16)
        wdn_bf[...] = wdn_ref[0, 0].astype(BF16)

    @pl.when(nv > 0)
    def _():
        w = xs_ref[...]
        rows = lax.broadcasted_iota(jnp.int32, (w.shape[0], 1), 0)
        w = jnp.where(rows < nv, w, jnp.uint32(0))
        x_lo, x_hi = _unpack_bf16_pairs(w)
        half = w.shape[1]
        au = (jnp.dot(x_lo, wgu_bf[0:half, :], preferred_element_type=F32)
              + jnp.dot(x_hi, wgu_bf[half:, :], preferred_element_type=F32))
        act = (jax.nn.silu(au[:, :ff]) * au[:, ff:]).astype(BF16)
        ys_ref[...] = _pack_bf16_pairs(jnp.dot(act, wdn_bf[...], preferred_element_type=F32))


def _experts(xs, w_gu, w_down, layer, block_e, block_i, block_nv, bm):
    n_buf = xs.shape[0]
    d, ff = w_gu.shape[2], w_down.shape[2]
    row = pl.BlockSpec((bm, d // 2), lambda i, be, bi, nv: (bi[i], 0))
    return pl.pallas_call(
        functools.partial(_expert_kernel, ff=ff),
        grid_spec=pltpu.PrefetchScalarGridSpec(
            num_scalar_prefetch=3,
            grid=(n_buf // bm,),
            in_specs=[row,
                      pl.BlockSpec((1, 1, d, 2 * ff), lambda i, be, bi, nv: (layer, be[i], 0, 0)),
                      pl.BlockSpec((1, 1, ff, d), lambda i, be, bi, nv: (layer, be[i], 0, 0))],
            out_specs=row,
            scratch_shapes=[pltpu.VMEM((d, 2 * ff), BF16), pltpu.VMEM((ff, d), BF16)]),
        out_shape=jax.ShapeDtypeStruct((n_buf, d // 2), jnp.uint32),
        compiler_params=_cparams(("arbitrary",), 56),
        name="moe_experts",
    )(block_e, block_i, block_nv, xs, w_gu, w_down)


def _combine_kernel(dest_ref, dnext_ref, ys_ref, gate_ref, h2_ref, x1_ref, mod_ref, wsg_ref, wsd_ref, gfin_ref,
                    out_ref, buf, sems, *, tc, ff, final_norm, whole_index):
    i = pl.program_id(0)
    n = pl.num_programs(0)
    slot = i % 2

    def start_tile(table, tok0, s):
        for t in range(tc):
            for r in range(TOP_K):
                pltpu.make_async_copy(ys_ref.at[pl.ds(table[r, tok0 + t], 1), :], buf.at[s, r, pl.ds(t, 1), :],
                                      sems.at[s, r]).start(priority=r % N_DMA_PRIORITIES)

    @pl.when(i == 0)
    def _():
        start_tile(dest_ref, 0, 0)

    @pl.when(i + 1 < n)
    def _():
        start_tile(dnext_ref, (i + 1) * tc if whole_index else 0, 1 - slot)

    acc = _swiglu(h2_ref[...].astype(BF16), wsg_ref[...], wsd_ref[...], ff)

    for r in range(TOP_K):
        for _ in range(tc):
            pltpu.make_async_copy(ys_ref.at[pl.ds(0, 1), :], buf.at[slot, r, pl.ds(0, 1), :], sems.at[slot, r]).wait()

    gate = gate_ref[...]
    routed_lo = routed_hi = None
    for r in range(TOP_K):
        w = buf[slot, r]
        lo = gate[:, r:r + 1] * lax.bitcast_convert_type(w << 16, F32)
        hi = gate[:, r:r + 1] * lax.bitcast_convert_type(w & jnp.uint32(0xFFFF0000), F32)
        routed_lo = lo if r == 0 else routed_lo + lo
        routed_hi = hi if r == 0 else routed_hi + hi
    routed = jnp.concatenate([routed_lo, routed_hi], axis=1)
    x2 = x1_ref[...] + mod_ref[0, 5:6, :] * (routed + acc)
    if final_norm:
        x2 = _rms(x2, gfin_ref[...])
    out_ref[...] = x2


def _combine(ys, dest_t, gates, h2, x1, mod, ws_gu_bf, ws_dn_bf, gain_final, seq, tc, final_norm):
    t, d = x1.shape
    ff = ws_dn_bf.shape[0]
    spt = seq // tc
    n = t // tc
    whole_index = tc % LANES != 0
    if whole_index:
        smem = smem_next = pl.BlockSpec((TOP_K, t), lambda i: (0, 0), memory_space=pltpu.SMEM)
    else:
        smem = pl.BlockSpec((TOP_K, tc), lambda i: (0, i), memory_space=pltpu.SMEM)
        smem_next = pl.BlockSpec((TOP_K, tc), lambda i: (0, jnp.minimum(i + 1, n - 1)), memory_space=pltpu.SMEM)
    row = pl.BlockSpec((tc, d), lambda i: (i, 0))
    return pl.pallas_call(
        functools.partial(_combine_kernel, tc=tc, ff=ff, final_norm=final_norm, whole_index=whole_index),
        grid=(n,),
        in_specs=[smem, smem_next,
                  pl.BlockSpec(memory_space=pl.ANY),
                  pl.BlockSpec((tc, TOP_K), lambda i: (i, 0)),
                  row, row,
                  pl.BlockSpec((1, N_ADA, d), lambda i: (i // spt, 0, 0)),
                  pl.BlockSpec((d, 2 * ff), lambda i: (0, 0)),
                  pl.BlockSpec((ff, d), lambda i: (0, 0)),
                  pl.BlockSpec((1, d), lambda i: (0, 0))],
        out_specs=row,
        out_shape=jax.ShapeDtypeStruct((t, d), F32),
        scratch_shapes=[pltpu.VMEM((2, TOP_K, tc, d // 2), jnp.uint32), pltpu.SemaphoreType.DMA((2, TOP_K))],
        compiler_params=_cparams(("arbitrary",), 56),
        name="moe_combine",
    )(dest_t, dest_t, ys, gates, h2, x1, mod, ws_gu_bf, ws_dn_bf, gain_final.reshape(1, d))


def _block_tables(counts, bm, n_blocks):
    n_e = counts.shape[0]
    padded = (counts + bm - 1) // bm * bm
    pad_end = jnp.cumsum(padded)
    pad_start = pad_end - padded
    used = pad_end[-1] // bm
    blk = jnp.arange(n_blocks, dtype=jnp.int32)
    src = jnp.minimum(blk, jnp.maximum(used - 1, 0))
    owner = jnp.sum((pad_end[None, :] <= (src * bm)[:, None]).astype(jnp.int32), axis=1)
    block_e = jnp.minimum(owner, n_e - 1)
    valid = jnp.clip(counts[block_e] - (src * bm - pad_start[block_e]), 0, bm)
    block_nv = jnp.where(blk < used, valid, 0).astype(jnp.int32)
    return pad_start.astype(jnp.int32), block_e.astype(jnp.int32), src.astype(jnp.int32), block_nv


def _moe(groups, bias, w_exp_gu, w_exp_down, layer, ws_gu_bf, ws_dn_bf, gain_final, tiles, final_norm):
    n_e = groups[0]["logits_t"].shape[0]
    sizes = [g["x1"].shape[0] for g in groups]
    t_all = sum(sizes)
    bm = tiles["bm"]
    logits_t = jnp.concatenate([g["logits_t"] for g in groups], axis=1)
    idx_t, gate_t, rank_t, cnt = _route(logits_t, bias, tiles["route"])
    counts = cnt[:, 0].astype(jnp.int32)
    n_blocks = -(-(t_all * TOP_K + n_e * (bm - 1)) // bm)
    pad_start, block_e, block_i, block_nv = _block_tables(counts, bm, n_blocks)
    dest_t = _dest_rows(idx_t, rank_t, pad_start, tiles["route"])
    offs = [sum(sizes[:k]) for k in range(len(groups))]
    xs = None
    for g, off, t in zip(groups, offs, sizes):
        xs = _dispatch(g["h2"], dest_t[:, off:off + t], n_blocks * bm, g["dispatch"], xs)
    ys = _experts(xs, w_exp_gu, w_exp_down, layer, block_e, block_i, block_nv, bm)
    return [_combine(ys, dest_t[:, off:off + t], gate_t[:, off:off + t].T, g["h2"], g["x1"], g["mod"], ws_gu_bf,
                     ws_dn_bf, gain_final, g["seq"], g["combine"], final_norm)
            for g, off, t in zip(groups, offs, sizes)]


PROMPT_TILES = dict(qkv=256, wo=256, pool=256, dispatch=256, combine=128)
SAMPLE_TILES = dict(qkv=32, wo=32, pool=32, dispatch=256, combine=32)
MOE_TILES = dict(route=256, bm=512)


def _mixer(st, i, p):
    b, s, d = st["shape"]
    x2d, mod, tiles = st["x2d"], st["mods"][i], st["tiles"]
    w_router = p["w_router"][i]
    if i % N_MIXERS == 0:
        a = i // N_MIXERS
        lam_init = 0.8 - 0.6 * math.exp(-0.3 * i)
        lam_vecs = jnp.stack([p["lambda_q1"][a], p["lambda_k1"][a], p["lambda_q2"][a], p["lambda_k2"][a]])
        q, k, v = _qkv(x2d, mod, p["norm_mix"][i], p["w_qkv_bf"][a], st["tabs"], s, tiles["qkv"])
        width = k.shape[1]
        if st["past_k"] is None:
            o = _attn_prompt(q, k, v, lam_vecs, p["subln_gain"][a], b, s, lam_init)
        else:
            o = _attn_sample(q, k, v, st["past_k"][a].reshape(-1, width), st["past_v"][a].reshape(-1, width),
                             lam_vecs, p["subln_gain"][a], b, s, lam_init)
        st["new_k"].append(k.reshape(b, s, width // HEAD_DIM, HEAD_DIM))
        st["new_v"].append(v.reshape(b, s, width // (2 * HEAD_DIM), 2 * HEAD_DIM))
        x1, h2, lg = _wo_post(o, p["w_o_bf"][a], x2d, mod, p["norm_ffn"][i], w_router, s, tiles["wo"])
    else:
        pi = i // N_MIXERS
        if st["past_pool"] is None:
            prev = jnp.zeros((b, POOL_MAX, d), F32)
        else:
            prev = jnp.concatenate([jnp.zeros((b, 1, d), F32), st["past_pool"][pi]], axis=1)
        x1, h2, lg, npool = _pool_post(x2d, prev, mod, p["norm_mix"][i], p["w_pool_bf"][pi], p["pool_scale"][pi],
                                       p["norm_ffn"][i], w_router, b, s, st["pos0"], tiles["pool"])
        st["new_pool"].append(npool[:, 1:, :])
    return dict(x1=x1, h2=h2, logits_t=lg, mod=mod, seq=s, dispatch=tiles["dispatch"], combine=tiles["combine"])


def _stream(x, mods, pos0, past_k, past_v, past_pool, tiles):
    b, s, d = x.shape
    return dict(shape=(b, s, d), x2d=x.reshape(b * s, d), mods=mods, pos0=pos0, tiles=tiles,
                tabs=_rope_tables(pos0 + jnp.arange(s, dtype=jnp.int32)),
                past_k=past_k, past_v=past_v, past_pool=past_pool, new_k=[], new_v=[], new_pool=[])


def kernel(x_prompt, x_sample, c_prompt, c_sample, cache_k, cache_v, state_pool, w_ada, b_ada, norm_mix, norm_ffn,
           norm_final, w_qkv, w_o, lambda_q1, lambda_k1, lambda_q2, lambda_k2, subln_gain, w_pool, pool_scale,
           w_router, router_bias, w_exp_gu, w_exp_down, w_sh_gu, w_sh_down):
    depth, d = norm_mix.shape
    nb = c_prompt.shape[0]
    mod_all = _ada(jnp.concatenate([c_prompt, c_sample], axis=0), w_ada, b_ada)
    mod_all = mod_all.reshape(depth, -1, N_ADA, d)
    p = dict(norm_mix=norm_mix, norm_ffn=norm_ffn, norm_final=norm_final,
             lambda_q1=lambda_q1, lambda_k1=lambda_k1, lambda_q2=lambda_q2, lambda_k2=lambda_k2,
             subln_gain=subln_gain, pool_scale=pool_scale, w_router=w_router, router_bias=router_bias,
             w_qkv_bf=w_qkv.astype(BF16), w_o_bf=w_o.astype(BF16), w_pool_bf=w_pool.astype(BF16),
             w_sh_gu_bf=w_sh_gu.astype(BF16), w_sh_down_bf=w_sh_down.astype(BF16))
    streams = [_stream(x_prompt, [mod_all[i, :nb] for i in range(depth)], 0, None, None, None, PROMPT_TILES),
               _stream(x_sample, [mod_all[i, nb:] for i in range(depth)], cache_k.shape[2], cache_k, cache_v,
                       state_pool, SAMPLE_TILES)]
    for i in range(depth):
        groups = [_mixer(st, i, p) for st in streams]
        outs = _moe(groups, router_bias[i], w_exp_gu, w_exp_down, i, p["w_sh_gu_bf"][i], p["w_sh_down_bf"][i],
                    norm_final, MOE_TILES, final_norm=(i == depth - 1))
        for st, x2d in zip(streams, outs):
            st["x2d"] = x2d
    res = [(st["x2d"].reshape(st["shape"]), jnp.stack(st["new_k"]), jnp.stack(st["new_v"]), jnp.stack(st["new_pool"]))
           for st in streams]
    (y_p, k_p, v_p, pool_p), (y_s, k_s, v_s, pool_s) = res
    return (y_p, y_s, k_p, v_p, pool_p, k_s, v_s, pool_s)
```

```python
import functools
import math

import jax
import jax.numpy as jnp
from jax import lax
from jax.experimental import pallas as pl
from jax.experimental.pallas import tpu as pltpu

CHUNK = 64
HEAD_DIM = 128
ROT_DIM = HEAD_DIM // 4
ROPE_THETA = 500000.0
POOL_WINDOWS = (2, 4, 8, 16)
POOL_MAX = 16
TOP_K = 8
ROUTED_SCALE = 2.5
NORM_EPS = 1e-6
N_ADA = 6
N_MIXERS = 2

LANES = 128
LOG2_E = math.log2(math.e)
MIB = 1024 * 1024

F32 = jnp.float32
BF16 = jnp.bfloat16
NT_DIMS = (((1,), (1,)), ((), ()))


def _cparams(sem, vmem_mib):
    return pltpu.CompilerParams(dimension_semantics=sem, vmem_limit_bytes=vmem_mib * MIB)


def _rms(x, gain):
    return x * lax.rsqrt(jnp.mean(x * x, axis=-1, keepdims=True) + NORM_EPS) * gain


def _norm_mod(x, gain, scale, shift):
    return _rms(x, gain) * (1.0 + scale) + shift


def _swiglu(x_bf, w_gu, w_down, ff):
    au = jnp.dot(x_bf, w_gu, preferred_element_type=F32)
    act = (jax.nn.silu(au[:, :ff]) * au[:, ff:]).astype(BF16)
    return jnp.dot(act, w_down, preferred_element_type=F32)


def _ada_kernel(c_ref, w_ref, b_ref, o_ref):
    c_act = jax.nn.silu(c_ref[...]).astype(BF16)
    o_ref[0] = jnp.dot(c_act, w_ref[0].astype(BF16), preferred_element_type=F32) + b_ref[0]


def _ada(c_all, w_ada, b_ada, tn=1024):
    depth, d, n = w_ada.shape
    r = c_all.shape[0]
    return pl.pallas_call(
        _ada_kernel,
        grid=(depth, n // tn),
        in_specs=[pl.BlockSpec((r, d), lambda i, j: (0, 0)),
                  pl.BlockSpec((1, d, tn), lambda i, j: (i, 0, j)),
                  pl.BlockSpec((1, 1, tn), lambda i, j: (i, 0, j))],
        out_specs=pl.BlockSpec((1, r, tn), lambda i, j: (i, 0, j)),
        out_shape=jax.ShapeDtypeStruct((depth, r, n), F32),
        compiler_params=_cparams(("arbitrary", "arbitrary"), 40),
        name="ada_mod",
    )(c_all, w_ada, b_ada.reshape(depth, 1, n))


def _qkv_kernel(x_ref, mod_ref, g_ref, w_hbm, cos_ref, sa_ref, sb_ref, q_ref, k_ref, v_ref, w_vmem, h_scr, sem, *, tn):
    @pl.when(pl.program_id(0) == 0)
    def _():
        load = pltpu.make_async_copy(w_hbm, w_vmem, sem)
        load.start()
        load.wait()

    h_scr[...] = _norm_mod(x_ref[...], g_ref[...], mod_ref[0, 1:2, :], mod_ref[0, 0:1, :]).astype(BF16)
    width = q_ref.shape[1]
    cos, sa, sb = cos_ref[...], sa_ref[...], sb_ref[...]

    def rope_store(acc, dst, off):
        for c in range(tn // HEAD_DIM):
            blk = acc[:, c * HEAD_DIM:(c + 1) * HEAD_DIM]
            rot = blk * cos + pltpu.roll(blk, HEAD_DIM - ROT_DIM // 2, 1) * sa + pltpu.roll(blk, ROT_DIM // 2, 1) * sb
            dst[:, off + c * HEAD_DIM:off + (c + 1) * HEAD_DIM] = rot.astype(dst.dtype)

    for j in range(3 * width // tn):
        acc = jnp.dot(h_scr[...], w_vmem[:, j * tn:(j + 1) * tn], preferred_element_type=F32)
        part, off = divmod(j * tn, width)
        if part == 0:
            rope_store(acc, q_ref, off)
        elif part == 1:
            rope_store(acc, k_ref, off)
        else:
            v_ref[:, off:off + tn] = acc


def _qkv(x2d, mod, gain, w_bf, tabs, seq, tm, tn=512):
    t, d = x2d.shape
    w3 = w_bf.shape[1]
    width = w3 // 3
    spt = seq // tm
    cos_t, sa_t, sb_t = tabs
    tab_spec = pl.BlockSpec((tm, HEAD_DIM), lambda i: (i % spt, 0))
    out_spec = pl.BlockSpec((tm, width), lambda i: (i, 0))
    return pl.pallas_call(
        functools.partial(_qkv_kernel, tn=tn),
        grid=(t // tm,),
        in_specs=[pl.BlockSpec((tm, d), lambda i: (i, 0)),
                  pl.BlockSpec((1, N_ADA, d), lambda i: (i // spt, 0, 0)),
                  pl.BlockSpec((1, d), lambda i: (0, 0)),
                  pl.BlockSpec(memory_space=pl.ANY),
                  tab_spec, tab_spec, tab_spec],
        out_specs=[out_spec, out_spec, out_spec],
        out_shape=[jax.ShapeDtypeStruct((t, width), BF16),
                   jax.ShapeDtypeStruct((t, width), F32),
                   jax.ShapeDtypeStruct((t, width), F32)],
        scratch_shapes=[pltpu.VMEM((d, w3), BF16), pltpu.VMEM((tm, d), BF16), pltpu.SemaphoreType.DMA(())],
        compiler_params=_cparams(("arbitrary",), 56),
        name="qkv_rope",
    )(x2d, mod, gain.reshape(1, d), w_bf, cos_t, sa_t, sb_t)


def _rope_tables(pos):
    half = ROT_DIM // 2
    inv_freq = ROPE_THETA ** (-jnp.arange(half, dtype=F32) / half)
    ang = pos.astype(F32)[:, None] * inv_freq[None, :]
    cos, sin = jnp.cos(ang), jnp.sin(ang)
    s = pos.shape[0]
    rest = HEAD_DIM - ROT_DIM
    cos_t = jnp.concatenate([cos, cos, jnp.ones((s, rest), F32)], axis=1)
    sa_t = jnp.concatenate([-sin, jnp.zeros((s, half + rest), F32)], axis=1)
    sb_t = jnp.concatenate([jnp.zeros((s, half), F32), sin, jnp.zeros((s, rest), F32)], axis=1)
    return cos_t, sa_t, sb_t


def _lam_value(lam_ref, lam_init):
    l1 = jnp.sum(lam_ref[0:1, :] * lam_ref[1:2, :], axis=1, keepdims=True)
    l2 = jnp.sum(lam_ref[2:3, :] * lam_ref[3:4, :], axis=1, keepdims=True)
    return jnp.exp(l1) - jnp.exp(l2) + lam_init


def _softmax_parts(parts, scale):
    mx = None
    for s in parts:
        m = jnp.max(s, axis=1, keepdims=True)
        mx = m if mx is None else jnp.maximum(mx, m)
    ps = [jnp.exp2((s - mx) * (scale * LOG2_E)) for s in parts]
    l = None
    for p in ps:
        r = jnp.sum(p, axis=1, keepdims=True)
        l = r if l is None else l + r
    return ps, l


def _subln(o, sub_ref, lam_init):
    return _rms(o, sub_ref[...]) * (1.0 - lam_init)


def _attn_prompt_kernel(q_ref, k_ref, v_ref, lam_ref, sub_ref, o_ref, kb, vb, *, tq, lam_init):
    seq = q_ref.shape[0]
    hd = HEAD_DIM
    scale = hd ** -0.5
    kb[...] = k_ref[...].astype(BF16)
    vb[...] = v_ref[...].astype(BF16)
    lam = _lam_value(lam_ref, lam_init)
    row = lax.broadcasted_iota(jnp.int32, (tq, tq), 0) // CHUNK
    col = lax.broadcasted_iota(jnp.int32, (tq, tq), 1) // CHUNK
    diag_mask = col <= row
    for qb in range(seq // tq):
        st, end = qb * tq, (qb + 1) * tq
        probs, coef = [], []
        for m in range(2):
            qm = q_ref[st:end, m * hd:(m + 1) * hd]
            km = kb[0:end, m * hd:(m + 1) * hd]
            s = lax.dot_general(qm, km, NT_DIMS, preferred_element_type=F32)
            parts = [s[:, :st]] if st > 0 else []
            parts.append(jnp.where(diag_mask, s[:, st:], -1e30))
            ps, l = _softmax_parts(parts, scale)
            probs.append(ps)
            coef.append((1.0 / l) if m == 0 else (lam / l))
        o = None
        bounds = ([(0, st)] if st > 0 else []) + [(st, end)]
        for idx, (lo, hi) in enumerate(bounds):
            a = (probs[0][idx] * coef[0] - probs[1][idx] * coef[1]).astype(BF16)
            part = jnp.dot(a, vb[lo:hi, :], preferred_element_type=F32)
            o = part if o is None else o + part
        o_ref[st:end, :] = _subln(o, sub_ref, lam_init).astype(o_ref.dtype)


def _attn_prompt(q, k, v, lam_vecs, subln, batch, seq, lam_init, tq=256):
    t, width = q.shape
    hw = 2 * HEAD_DIM
    spec = pl.BlockSpec((seq, hw), lambda b, h: (b, h))
    return pl.pallas_call(
        functools.partial(_attn_prompt_kernel, tq=tq, lam_init=lam_init),
        grid=(batch, width // hw),
        in_specs=[spec, spec, spec,
                  pl.BlockSpec((4, HEAD_DIM), lambda b, h: (0, 0)),
                  pl.BlockSpec((1, hw), lambda b, h: (0, 0))],
        out_specs=spec,
        out_shape=jax.ShapeDtypeStruct((t, width), BF16),
        scratch_shapes=[pltpu.VMEM((seq, hw), BF16), pltpu.VMEM((seq, hw), BF16)],
        compiler_params=_cparams(("arbitrary", "arbitrary"), 56),
        name="attn_prompt",
    )(q, k, v, lam_vecs, subln.reshape(1, hw))


def _attn_sample_kernel(q_ref, k_ref, v_ref, pk_ref, pv_ref, lam_ref, sub_ref, o_ref, *, lam_init):
    hd = HEAD_DIM
    scale = hd ** -0.5
    lam = _lam_value(lam_ref, lam_init)
    pk = pk_ref[...].astype(BF16)
    kn = k_ref[...].astype(BF16)
    probs, coef = [], []
    for m in range(2):
        qm = q_ref[:, m * hd:(m + 1) * hd]
        parts = [lax.dot_general(qm, pk[:, m * hd:(m + 1) * hd], NT_DIMS, preferred_element_type=F32),
                 lax.dot_general(qm, kn[:, m * hd:(m + 1) * hd], NT_DIMS, preferred_element_type=F32)]
        ps, l = _softmax_parts(parts, scale)
        probs.append(ps)
        coef.append((1.0 / l) if m == 0 else (lam / l))
    vals = [pv_ref[...].astype(BF16), v_ref[...].astype(BF16)]
    o = None
    for idx in range(2):
        a = (probs[0][idx] * coef[0] - probs[1][idx] * coef[1]).astype(BF16)
        part = jnp.dot(a, vals[idx], preferred_element_type=F32)
        o = part if o is None else o + part
    o_ref[...] = _subln(o, sub_ref, lam_init).astype(o_ref.dtype)


def _attn_sample(q, k, v, past_k, past_v, lam_vecs, subln, batch, seq, lam_init):
    t, width = q.shape
    hw = 2 * HEAD_DIM
    past = past_k.shape[0] // batch
    spec = pl.BlockSpec((seq, hw), lambda b, h: (b, h))
    pspec = pl.BlockSpec((past, hw), lambda b, h: (b, h))
    return pl.pallas_call(
        functools.partial(_attn_sample_kernel, lam_init=lam_init),
        grid=(batch, width // hw),
        in_specs=[spec, spec, spec, pspec, pspec,
                  pl.BlockSpec((4, HEAD_DIM), lambda b, h: (0, 0)),
                  pl.BlockSpec((1, hw), lambda b, h: (0, 0))],
        out_specs=spec,
        out_shape=jax.ShapeDtypeStruct((t, width), BF16),
        compiler_params=_cparams(("arbitrary", "arbitrary"), 48),
        name="attn_sample",
    )(q, k, v, past_k, past_v, lam_vecs, subln.reshape(1, hw))


def _post_mixer(x, y, mod_ref, gf_ref, wr_ref, x1_ref, h2_ref, lg_ref):
    x1 = x + mod_ref[0, 2:3, :] * y
    h2 = _norm_mod(x1, gf_ref[...], mod_ref[0, 4:5, :], mod_ref[0, 3:4, :])
    x1_ref[...] = x1
    h2_ref[...] = h2
    h_hi = h2.astype(BF16)
    h_lo = (h2 - h_hi.astype(F32)).astype(BF16)
    w = wr_ref[...]
    w_hi = w.astype(BF16)
    w_lo = (w - w_hi.astype(F32)).astype(BF16)
    lg_ref[...] = (jnp.dot(h_hi, w_hi, preferred_element_type=F32)
                   + (jnp.dot(h_hi, w_lo, preferred_element_type=F32)
                      + jnp.dot(h_lo, w_hi, preferred_element_type=F32)))


def _wo_kernel(o_ref, w_ref, x_ref, mod_ref, gf_ref, wr_ref, x1_ref, h2_ref, lg_ref):
    y = jnp.dot(o_ref[...], w_ref[...], preferred_element_type=F32)
    _post_mixer(x_ref[...], y, mod_ref, gf_ref, wr_ref, x1_ref, h2_ref, lg_ref)


def _wo_post(o, w_bf, x2d, mod, gain_ffn, w_router, seq, tm):
    t, d = x2d.shape
    width = o.shape[1]
    e = w_router.shape[1]
    spt = seq // tm
    row = pl.BlockSpec((tm, d), lambda i: (i, 0))
    x1, h2, lg = pl.pallas_call(
        _wo_kernel,
        grid=(t // tm,),
        in_specs=[pl.BlockSpec((tm, width), lambda i: (i, 0)),
                  pl.BlockSpec((width, d), lambda i: (0, 0)),
                  row,
                  pl.BlockSpec((1, N_ADA, d), lambda i: (i // spt, 0, 0)),
                  pl.BlockSpec((1, d), lambda i: (0, 0)),
                  pl.BlockSpec((d, e), lambda i: (0, 0))],
        out_specs=[row, row, pl.BlockSpec((tm, e), lambda i: (i, 0))],
        out_shape=[jax.ShapeDtypeStruct((t, d), F32), jax.ShapeDtypeStruct((t, d), F32),
                   jax.ShapeDtypeStruct((t, e), F32)],
        compiler_params=_cparams(("arbitrary",), 48),
        name="wo_post",
    )(o, w_bf, x2d, mod, gain_ffn.reshape(1, d), w_router)
    return x1, h2, lg.T


def _pool_kernel(x_ref, xh_ref, prev_ref, mod_ref, gm_ref, wp_ref, ps_ref, gf_ref, wr_ref,
                 x1_ref, h2_ref, lg_ref, np_ref, hext, ybuf, *, pos0, tm):
    s = pl.program_id(1)
    halo = POOL_MAX
    x = x_ref[...]
    sc1, sh1 = mod_ref[0, 1:2, :], mod_ref[0, 0:1, :]
    h = _norm_mod(x, gm_ref[...], sc1, sh1)
    h_halo = _norm_mod(xh_ref[...], gm_ref[...], sc1, sh1)
    hext[0:halo, :] = jnp.where(s == 0, prev_ref[0], h_halo)
    hext[halo:, :] = h
    pos = pos0 + s * tm + lax.broadcasted_iota(jnp.int32, (tm, 1), 0)
    gc = x.shape[1] // len(POOL_WINDOWS)
    for g, w in enumerate(POOL_WINDOWS):
        c0, c1 = g * gc, (g + 1) * gc
        tot = hext[halo:halo + tm, c0:c1]
        for k in range(1, w):
            tot = tot + hext[halo - k:halo - k + tm, c0:c1]
        cnt = jnp.minimum(w, pos + 1).astype(F32)
        dlt = tot / cnt - hext[halo:halo + tm, c0:c1]
        yg = jnp.dot(dlt.astype(BF16), wp_ref[g], preferred_element_type=F32)
        ybuf[:, c0:c1] = yg * ps_ref[:, c0:c1]
    _post_mixer(x, ybuf[...], mod_ref, gf_ref, wr_ref, x1_ref, h2_ref, lg_ref)

    @pl.when(s == pl.num_programs(1) - 1)
    def _():
        np_ref[0] = hext[tm:tm + halo, :]


def _pool_post(x2d, prev, mod, gain_mix, wp_bf, pool_scale, gain_ffn, w_router, batch, seq, pos0, tm):
    t, d = x2d.shape
    e = w_router.shape[1]
    spt = seq // tm
    hpt = tm // POOL_MAX
    ng, gc = wp_bf.shape[0], wp_bf.shape[1]
    row = pl.BlockSpec((tm, d), lambda b, s: (b * spt + s, 0))
    vec = pl.BlockSpec((1, d), lambda b, s: (0, 0))
    x1, h2, lg, new_pool = pl.pallas_call(
        functools.partial(_pool_kernel, pos0=pos0, tm=tm),
        grid=(batch, spt),
        in_specs=[row,
                  pl.BlockSpec((POOL_MAX, d), lambda b, s: (jnp.maximum((b * spt + s) * hpt - 1, 0), 0)),
                  pl.BlockSpec((1, POOL_MAX, d), lambda b, s: (b, 0, 0)),
                  pl.BlockSpec((1, N_ADA, d), lambda b, s: (b, 0, 0)),
                  vec,
                  pl.BlockSpec((ng, gc, gc), lambda b, s: (0, 0, 0)),
                  vec, vec,
                  pl.BlockSpec((d, e), lambda b, s: (0, 0))],
        out_specs=[row, row, pl.BlockSpec((tm, e), lambda b, s: (b * spt + s, 0)),
                   pl.BlockSpec((1, POOL_MAX, d), lambda b, s: (b, 0, 0))],
        out_shape=[jax.ShapeDtypeStruct((t, d), F32), jax.ShapeDtypeStruct((t, d), F32),
                   jax.ShapeDtypeStruct((t, e), F32), jax.ShapeDtypeStruct((batch, POOL_MAX, d), F32)],
        scratch_shapes=[pltpu.VMEM((POOL_MAX + tm, d), F32), pltpu.VMEM((tm, d), F32)],
        compiler_params=_cparams(("arbitrary", "arbitrary"), 48),
        name="pool_post",
    )(x2d, x2d, prev, mod, gain_mix.reshape(1, d), wp_bf, pool_scale.reshape(1, d), gain_ffn.reshape(1, d), w_router)
    return x1, h2, lg.T, new_pool


def _route_kernel(lg_ref, bias_ref, idx_ref, gate_ref, rank_ref, cnt_ref, carry):
    i = pl.program_id(0)

    @pl.when(i == 0)
    def _():
        carry[...] = jnp.zeros_like(carry)

    score = jax.nn.sigmoid(lg_ref[...])
    n_e, tm = score.shape
    val = score + bias_ref[...]
    e_iota = lax.broadcasted_iota(jnp.int32, (n_e, tm), 0)
    sel = jnp.zeros((n_e, tm), F32)
    idxs, picked = [], []
    for _ in range(TOP_K):
        best = jnp.max(val, axis=0, keepdims=True)
        ix = jnp.min(jnp.where(val == best, e_iota, n_e), axis=0, keepdims=True)
        hit = e_iota == ix
        picked.append(jnp.sum(jnp.where(hit, score, 0.0), axis=0, keepdims=True))
        idxs.append(ix)
        val = jnp.where(hit, -jnp.inf, val)
        sel = jnp.where(hit, 1.0, sel)
    tri = (lax.broadcasted_iota(jnp.int32, (tm, tm), 0) <= lax.broadcasted_iota(jnp.int32, (tm, tm), 1)).astype(BF16)
    incl = jnp.dot(sel.astype(BF16), tri, preferred_element_type=F32)
    excl = incl - sel + carry[:, 0:1]
    total = picked[0]
    for p in picked[1:]:
        total = total + p
    for r in range(TOP_K):
        hit = e_iota == idxs[r]
        idx_ref[r:r + 1, :] = idxs[r]
        gate_ref[r:r + 1, :] = picked[r] / total * ROUTED_SCALE
        rank_ref[r:r + 1, :] = jnp.sum(jnp.where(hit, excl, 0.0), axis=0, keepdims=True).astype(jnp.int32)
    carry[...] = carry[...] + jnp.sum(sel, axis=1, keepdims=True)
    cnt_ref[...] = carry[...]


def _route(logits_t, bias, tm):
    e, t = logits_t.shape
    out = pl.BlockSpec((TOP_K, tm), lambda i: (0, i))
    return pl.pallas_call(
        _route_kernel,
        grid=(t // tm,),
        in_specs=[pl.BlockSpec((e, tm), lambda i: (0, i)), pl.BlockSpec((e, 1), lambda i: (0, 0))],
        out_specs=[out, out, out, pl.BlockSpec((e, LANES), lambda i: (0, 0))],
        out_shape=[jax.ShapeDtypeStruct((TOP_K, t), jnp.int32), jax.ShapeDtypeStruct((TOP_K, t), F32),
                   jax.ShapeDtypeStruct((TOP_K, t), jnp.int32), jax.ShapeDtypeStruct((e, LANES), F32)],
        scratch_shapes=[pltpu.VMEM((e, LANES), F32)],
        compiler_params=_cparams(("arbitrary",), 32),
        name="route_topk",
    )(logits_t, bias.reshape(e, 1))


DISPATCH_CHUNK = 32
N_DMA_PRIORITIES = 2


def _row_copy(src, src_row, dst, dst_row, sem):
    return pltpu.make_async_copy(src.at[pl.ds(src_row, 1), :], dst.at[pl.ds(dst_row, 1), :], sem)


def _dest_kernel(ps_ref, idx_ref, rank_ref, dest_ref):
    idx = idx_ref[...]
    dest = rank_ref[...]
    for e in range(ps_ref.shape[0]):
        dest = dest + jnp.where(idx == e, ps_ref[e], 0)
    dest_ref[...] = dest


def _dest_rows(idx_t, rank_t, pad_start, tm):
    k, t = idx_t.shape
    blk = pl.BlockSpec((k, tm), lambda i, ps: (0, i))
    return pl.pallas_call(
        _dest_kernel,
        grid_spec=pltpu.PrefetchScalarGridSpec(num_scalar_prefetch=1, grid=(t // tm,), in_specs=[blk, blk],
                                               out_specs=blk),
        out_shape=jax.ShapeDtypeStruct((k, t), jnp.int32),
        compiler_params=_cparams(("arbitrary",), 32),
        name="moe_dest",
    )(pad_start, idx_t, rank_t)


def _pack_bf16_pairs(x):
    half = x.shape[1] // 2
    lo = lax.bitcast_convert_type(x[:, :half].astype(BF16).astype(F32), jnp.uint32)
    hi = lax.bitcast_convert_type(x[:, half:].astype(BF16).astype(F32), jnp.uint32)
    return (lo >> 16) | (hi & jnp.uint32(0xFFFF0000))


def _unpack_bf16_pairs(w):
    lo = lax.bitcast_convert_type(w << 16, F32).astype(BF16)
    hi = lax.bitcast_convert_type(w & jnp.uint32(0xFFFF0000), F32).astype(BF16)
    return lo, hi


def _dispatch_kernel(dest_ref, h_ref, xs_ref, packed, sems, *, td):
    i = pl.program_id(0)
    slot = i % 2
    n_chunks = td // DISPATCH_CHUNK
    packed[slot] = _pack_bf16_pairs(h_ref[...])

    for s in range(2):
        @pl.when(slot == s)
        def _():
            for c in range(n_chunks):
                for t in range(c * DISPATCH_CHUNK, (c + 1) * DISPATCH_CHUNK):
                    for r in range(TOP_K):
                        _row_copy(packed.at[s], t, xs_ref, dest_ref[r, t], sems.at[s, c]).start(
                            priority=r % N_DMA_PRIORITIES)

    def wait_step(s):
        for c in range(n_chunks):
            for _ in range(DISPATCH_CHUNK * TOP_K):
                _row_copy(packed.at[s], 0, xs_ref, 0, sems.at[s, c]).wait()

    @pl.when(i > 0)
    def _():
        wait_step(1 - slot)

    @pl.when(i == pl.num_programs(0) - 1)
    def _():
        wait_step(slot)


def _dispatch_into_kernel(dest_ref, h_ref, xs_in_ref, xs_ref, packed, sems, *, td):
    del xs_in_ref
    _dispatch_kernel(dest_ref, h_ref, xs_ref, packed, sems, td=td)


def _dispatch(h2, dest_t, n_buf, td, xs_prev=None):
    t, d = h2.shape
    in_specs = [pl.BlockSpec((TOP_K, td), lambda i: (0, i), memory_space=pltpu.SMEM),
                pl.BlockSpec((td, d), lambda i: (i, 0))]
    args = [dest_t, h2]
    kern = functools.partial(_dispatch_kernel, td=td)
    aliases = {}
    if xs_prev is not None:
        in_specs.append(pl.BlockSpec(memory_space=pl.ANY))
        args.append(xs_prev)
        aliases = {2: 0}
        kern = functools.partial(_dispatch_into_kernel, td=td)
    return pl.pallas_call(
        kern,
        grid=(t // td,),
        in_specs=in_specs,
        out_specs=pl.BlockSpec(memory_space=pl.ANY),
        out_shape=jax.ShapeDtypeStruct((n_buf, d // 2), jnp.uint32),
        scratch_shapes=[pltpu.VMEM((2, td, d // 2), jnp.uint32),
                        pltpu.SemaphoreType.DMA((2, td // DISPATCH_CHUNK))],
        input_output_aliases=aliases,
        compiler_params=_cparams(("arbitrary",), 32),
        name="moe_dispatch",
    )(*args)


def _expert_kernel(be_ref, bi_ref, nv_ref, xs_ref, wgu_ref, wdn_ref, ys_ref, wgu_bf, wdn_bf, *, ff):
    i = pl.program_id(0)
    nv = nv_ref[i]

    @pl.when((i == 0) | (be_ref[i] != be_ref[jnp.maximum(i - 1, 0)]))
    def _():
        wgu_bf[...] = wgu_ref[0, 0].astype(BF16)
        wdn_bf[...] = wdn_ref[0, 0].astype(BF16)

    @pl.when(nv > 0)
    def _():
        w = xs_ref[...]
        rows = lax.broadcasted_iota(jnp.int32, (w.shape[0], 1), 0)
        w = jnp.where(rows < nv, w, jnp.uint32(0))
        x_lo, x_hi = _unpack_bf16_pairs(w)
        half = w.shape[1]
        au = (jnp.dot(x_lo, wgu_bf[0:half, :], preferred_element_type=F32)
              + jnp.dot(x_hi, wgu_bf[half:, :], preferred_element_type=F32))
        act = (jax.nn.silu(au[:, :ff]) * au[:, ff:]).astype(BF16)
        ys_ref[...] = _pack_bf16_pairs(jnp.dot(act, wdn_bf[...], preferred_element_type=F32))


def _experts(xs, w_gu, w_down, layer, block_e, block_i, block_nv, bm):
    n_buf = xs.shape[0]
    d, ff = w_gu.shape[2], w_down.shape[2]
    row = pl.BlockSpec((bm, d // 2), lambda i, be, bi, nv: (bi[i], 0))
    return pl.pallas_call(
        functools.partial(_expert_kernel, ff=ff),
        grid_spec=pltpu.PrefetchScalarGridSpec(
            num_scalar_prefetch=3,
            grid=(n_buf // bm,),
            in_specs=[row,
                      pl.BlockSpec((1, 1, d, 2 * ff), lambda i, be, bi, nv: (layer, be[i], 0, 0)),
                      pl.BlockSpec((1, 1, ff, d), lambda i, be, bi, nv: (layer, be[i], 0, 0))],
            out_specs=row,
            scratch_shapes=[pltpu.VMEM((d, 2 * ff), BF16), pltpu.VMEM((ff, d), BF16)]),
        out_shape=jax.ShapeDtypeStruct((n_buf, d // 2), jnp.uint32),
        compiler_params=_cparams(("arbitrary",), 56),
        name="moe_experts",
    )(block_e, block_i, block_nv, xs, w_gu, w_down)


def _combine_kernel(dest_ref, dnext_ref, ys_ref, gate_ref, h2_ref, x1_ref, mod_ref, wsg_ref, wsd_ref, gfin_ref,
                    out_ref, buf, sems, *, tc, ff, final_norm, whole_index):
    i = pl.program_id(0)
    n = pl.num_programs(0)
    slot = i % 2

    def start_tile(table, tok0, s):
        for t in range(tc):
            for r in range(TOP_K):
                pltpu.make_async_copy(ys_ref.at[pl.ds(table[r, tok0 + t], 1), :], buf.at[s, r, pl.ds(t, 1), :],
                                      sems.at[s, r]).start(priority=r % N_DMA_PRIORITIES)

    @pl.when(i == 0)
    def _():
        start_tile(dest_ref, 0, 0)

    for s in range(2):
        @pl.when((i + 1 < n) & (slot == 1 - s))
        def _():
            start_tile(dnext_ref, (i + 1) * tc if whole_index else 0, s)

    acc = _swiglu(h2_ref[...].astype(BF16), wsg_ref[...], wsd_ref[...], ff)

    for r in range(TOP_K):
        for _ in range(tc):
            pltpu.make_async_copy(ys_ref.at[pl.ds(0, 1), :], buf.at[slot, r, pl.ds(0, 1), :], sems.at[slot, r]).wait()

    gate = gate_ref[...]
    routed_lo = routed_hi = None
    for r in range(TOP_K):
        w = buf[slot, r]
        lo = gate[:, r:r + 1] * lax.bitcast_convert_type(w << 16, F32)
        hi = gate[:, r:r + 1] * lax.bitcast_convert_type(w & jnp.uint32(0xFFFF0000), F32)
        routed_lo = lo if r == 0 else routed_lo + lo
        routed_hi = hi if r == 0 else routed_hi + hi
    routed = jnp.concatenate([routed_lo, routed_hi], axis=1)
    x2 = x1_ref[...] + mod_ref[0, 5:6, :] * (routed + acc)
    if final_norm:
        x2 = _rms(x2, gfin_ref[...])
    out_ref[...] = x2


def _combine(ys, dest_t, gates, h2, x1, mod, ws_gu_bf, ws_dn_bf, gain_final, seq, tc, final_norm):
    t, d = x1.shape
    ff = ws_dn_bf.shape[0]
    spt = seq // tc
    n = t // tc
    whole_index = tc % LANES != 0
    if whole_index:
        smem = smem_next = pl.BlockSpec((TOP_K, t), lambda i: (0, 0), memory_space=pltpu.SMEM)
    else:
        smem = pl.BlockSpec((TOP_K, tc), lambda i: (0, i), memory_space=pltpu.SMEM)
        smem_next = pl.BlockSpec((TOP_K, tc), lambda i: (0, jnp.minimum(i + 1, n - 1)), memory_space=pltpu.SMEM)
    row = pl.BlockSpec((tc, d), lambda i: (i, 0))
    return pl.pallas_call(
        functools.partial(_combine_kernel, tc=tc, ff=ff, final_norm=final_norm, whole_index=whole_index),
        grid=(n,),
        in_specs=[smem, smem_next,
                  pl.BlockSpec(memory_space=pl.ANY),
                  pl.BlockSpec((tc, TOP_K), lambda i: (i, 0)),
                  row, row,
                  pl.BlockSpec((1, N_ADA, d), lambda i: (i // spt, 0, 0)),
                  pl.BlockSpec((d, 2 * ff), lambda i: (0, 0)),
                  pl.BlockSpec((ff, d), lambda i: (0, 0)),
                  pl.BlockSpec((1, d), lambda i: (0, 0))],
        out_specs=row,
        out_shape=jax.ShapeDtypeStruct((t, d), F32),
        scratch_shapes=[pltpu.VMEM((2, TOP_K, tc, d // 2), jnp.uint32), pltpu.SemaphoreType.DMA((2, TOP_K))],
        compiler_params=_cparams(("arbitrary",), 56),
        name="moe_combine",
    )(dest_t, dest_t, ys, gates, h2, x1, mod, ws_gu_bf, ws_dn_bf, gain_final.reshape(1, d))


def _block_tables(counts, bm, n_blocks):
    n_e = counts.shape[0]
    padded = (counts + bm - 1) // bm * bm
    pad_end = jnp.cumsum(padded)
    pad_start = pad_end - padded
    used = pad_end[-1] // bm
    blk = jnp.arange(n_blocks, dtype=jnp.int32)
    src = jnp.minimum(blk, jnp.maximum(used - 1, 0))
    owner = jnp.sum((pad_end[None, :] <= (src * bm)[:, None]).astype(jnp.int32), axis=1)
    block_e = jnp.minimum(owner, n_e - 1)
    valid = jnp.clip(counts[block_e] - (src * bm - pad_start[block_e]), 0, bm)
    block_nv = jnp.where(blk < used, valid, 0).astype(jnp.int32)
    return pad_start.astype(jnp.int32), block_e.astype(jnp.int32), src.astype(jnp.int32), block_nv


def _moe(groups, bias, w_exp_gu, w_exp_down, layer, ws_gu_bf, ws_dn_bf, gain_final, tiles, final_norm):
    n_e = groups[0]["logits_t"].shape[0]
    sizes = [g["x1"].shape[0] for g in groups]
    t_all = sum(sizes)
    bm = tiles["bm"]
    logits_t = jnp.concatenate([g["logits_t"] for g in groups], axis=1)
    idx_t, gate_t, rank_t, cnt = _route(logits_t, bias, tiles["route"])
    counts = cnt[:, 0].astype(jnp.int32)
    n_blocks = -(-(t_all * TOP_K + n_e * (bm - 1)) // bm)
    pad_start, block_e, block_i, block_nv = _block_tables(counts, bm, n_blocks)
    dest_t = _dest_rows(idx_t, rank_t, pad_start, tiles["route"])
    offs = [sum(sizes[:k]) for k in range(len(groups))]
    xs = None
    for g, off, t in zip(groups, offs, sizes):
        xs = _dispatch(g["h2"], dest_t[:, off:off + t], n_blocks * bm, g["dispatch"], xs)
    ys = _experts(xs, w_exp_gu, w_exp_down, layer, block_e, block_i, block_nv, bm)
    return [_combine(ys, dest_t[:, off:off + t], gate_t[:, off:off + t].T, g["h2"], g["x1"], g["mod"], ws_gu_bf,
                     ws_dn_bf, gain_final, g["seq"], g["combine"], final_norm)
            for g, off, t in zip(groups, offs, sizes)]


PROMPT_TILES = dict(qkv=256, wo=256, pool=256, dispatch=256, combine=256)
SAMPLE_TILES = dict(qkv=32, wo=32, pool=32, dispatch=256, combine=32)
MOE_TILES = dict(route=256, bm=512)


def _mixer(st, i, p):
    b, s, d = st["shape"]
    x2d, mod, tiles = st["x2d"], st["mods"][i], st["tiles"]
    w_router = p["w_router"][i]
    if i % N_MIXERS == 0:
        a = i // N_MIXERS
        lam_init = 0.8 - 0.6 * math.exp(-0.3 * i)
        lam_vecs = jnp.stack([p["lambda_q1"][a], p["lambda_k1"][a], p["lambda_q2"][a], p["lambda_k2"][a]])
        q, k, v = _qkv(x2d, mod, p["norm_mix"][i], p["w_qkv_bf"][a], st["tabs"], s, tiles["qkv"])
        width = k.shape[1]
        if st["past_k"] is None:
            o = _attn_prompt(q, k, v, lam_vecs, p["subln_gain"][a], b, s, lam_init)
        else:
            o = _attn_sample(q, k, v, st["past_k"][a].reshape(-1, width), st["past_v"][a].reshape(-1, width),
                             lam_vecs, p["subln_gain"][a], b, s, lam_init)
        st["new_k"].append(k.reshape(b, s, width // HEAD_DIM, HEAD_DIM))
        st["new_v"].append(v.reshape(b, s, width // (2 * HEAD_DIM), 2 * HEAD_DIM))
        x1, h2, lg = _wo_post(o, p["w_o_bf"][a], x2d, mod, p["norm_ffn"][i], w_router, s, tiles["wo"])
    else:
        pi = i // N_MIXERS
        if st["past_pool"] is None:
            prev = jnp.zeros((b, POOL_MAX, d), F32)
        else:
            prev = jnp.concatenate([jnp.zeros((b, 1, d), F32), st["past_pool"][pi]], axis=1)
        x1, h2, lg, npool = _pool_post(x2d, prev, mod, p["norm_mix"][i], p["w_pool_bf"][pi], p["pool_scale"][pi],
                                       p["norm_ffn"][i], w_router, b, s, st["pos0"], tiles["pool"])
        st["new_pool"].append(npool[:, 1:, :])
    return dict(x1=x1, h2=h2, logits_t=lg, mod=mod, seq=s, dispatch=tiles["dispatch"], combine=tiles["combine"])


def _stream(x, mods, pos0, past_k, past_v, past_pool, tiles):
    b, s, d = x.shape
    return dict(shape=(b, s, d), x2d=x.reshape(b * s, d), mods=mods, pos0=pos0, tiles=tiles,
                tabs=_rope_tables(pos0 + jnp.arange(s, dtype=jnp.int32)),
                past_k=past_k, past_v=past_v, past_pool=past_pool, new_k=[], new_v=[], new_pool=[])


def kernel(x_prompt, x_sample, c_prompt, c_sample, cache_k, cache_v, state_pool, w_ada, b_ada, norm_mix, norm_ffn,
           norm_final, w_qkv, w_o, lambda_q1, lambda_k1, lambda_q2, lambda_k2, subln_gain, w_pool, pool_scale,
           w_router, router_bias, w_exp_gu, w_exp_down, w_sh_gu, w_sh_down):
    depth, d = norm_mix.shape
    nb = c_prompt.shape[0]
    mod_all = _ada(jnp.concatenate([c_prompt, c_sample], axis=0), w_ada, b_ada)
    mod_all = mod_all.reshape(depth, -1, N_ADA, d)
    p = dict(norm_mix=norm_mix, norm_ffn=norm_ffn, norm_final=norm_final,
             lambda_q1=lambda_q1, lambda_k1=lambda_k1, lambda_q2=lambda_q2, lambda_k2=lambda_k2,
             subln_gain=subln_gain, pool_scale=pool_scale, w_router=w_router, router_bias=router_bias,
             w_qkv_bf=w_qkv.astype(BF16), w_o_bf=w_o.astype(BF16), w_pool_bf=w_pool.astype(BF16),
             w_sh_gu_bf=w_sh_gu.astype(BF16), w_sh_down_bf=w_sh_down.astype(BF16))
    streams = [_stream(x_prompt, [mod_all[i, :nb] for i in range(depth)], 0, None, None, None, PROMPT_TILES),
               _stream(x_sample, [mod_all[i, nb:] for i in range(depth)], cache_k.shape[2], cache_k, cache_v,
                       state_pool, SAMPLE_TILES)]
    for i in range(depth):
        groups = [_mixer(st, i, p) for st in streams]
        outs = _moe(groups, router_bias[i], w_exp_gu, w_exp_down, i, p["w_sh_gu_bf"][i], p["w_sh_down_bf"][i],
                    norm_final, MOE_TILES, final_norm=(i == depth - 1))
        for st, x2d in zip(streams, outs):
            st["x2d"] = x2d
    res = [(st["x2d"].reshape(st["shape"]), jnp.stack(st["new_k"]), jnp.stack(st["new_v"]), jnp.stack(st["new_pool"]))
           for st in streams]
    (y_p, k_p, v_p, pool_p), (y_s, k_s, v_s, pool_s) = res
    return (y_p, y_s, k_p, v_p, pool_p, k_s, v_s, pool_s)
```
